```python
import math
import jax, jax.numpy as jnp
from jax import lax
import numpy as np

D_MODEL = 2048
BATCH = 4
SEQ = 2048
DEPTH = 4
DEC_BATCH = 32
DEC_SEQ = 8
PAST_LEN = 16384
PAGE_SIZE = 128

D_MIX = D_MODEL
HEAD_DIM = 64
D_ATTN = D_MIX // 2
N_Q_HEADS = D_ATTN // HEAD_DIM
N_KV_HEADS = 4
Q_PER_KV = N_Q_HEADS // N_KV_HEADS
D_KV = N_KV_HEADS * HEAD_DIM
WINDOW = 128
ROPE_THETA = 10000.0
D_SGU = D_MIX // 4
N_SGU_HEADS = 4
SGU_HEAD_DIM = D_SGU // N_SGU_HEADS
SGU_CHUNK = 128
D_SSM = D_MIX - D_ATTN - D_SGU
SSM_GROUP = 16
N_SSM_GROUPS = D_SSM // SSM_GROUP
SSM_STATE = 64
D_IN_PROJ = D_ATTN + 2 * D_KV + 2 * D_SGU + D_SSM
D_FF = 5632
N_EXPERTS = 8
TOP_K = 2
D_FF_EXPERT = 7168
N_DENSE = (DEPTH + 1) // 2
N_MOE = DEPTH // 2
NORM_EPS = 1e-6
NEG_INF = -1e30

kernel_name = 'hymba_style_swa_sgu_s5_moe_step'


def rms_scale(x):
    xf = x.astype(jnp.float32)
    return xf * lax.rsqrt(jnp.mean(xf * xf, axis=-1, keepdims=True) + NORM_EPS)


def rms_norm(x, g):
    return (rms_scale(x) * g.astype(jnp.float32)).astype(x.dtype)


def layer_norm(x, g, b):
    xf = x.astype(jnp.float32)
    xc = xf - jnp.mean(xf, axis=-1, keepdims=True)
    y = xc * lax.rsqrt(jnp.mean(xc * xc, axis=-1, keepdims=True) + NORM_EPS)
    return (y * g.astype(jnp.float32) + b.astype(jnp.float32)).astype(x.dtype)


def rope(x, pos):
    half = HEAD_DIM // 2
    inv_freq = jnp.power(ROPE_THETA, -jnp.arange(half, dtype=jnp.float32) / half)
    ang = pos.astype(jnp.float32)[:, None] * inv_freq[None, :]
    cos = jnp.cos(ang)[None, :, None, :]
    sin = jnp.sin(ang)[None, :, None, :]
    xf = x.astype(jnp.float32)
    x1, x2 = xf[..., :half], xf[..., half:]
    return jnp.concatenate([x1 * cos - x2 * sin, x1 * sin + x2 * cos], axis=-1).astype(x.dtype)


def sink_attention(q, k, v, mask, sinks):
    s = jnp.einsum('bnqkgd,bnskd->bnkgqs', q, k).astype(jnp.float32) * (HEAD_DIM ** -0.5)
    s = jnp.where(mask[None, :, None, None], s, NEG_INF)
    sink = sinks.astype(jnp.float32).reshape(N_KV_HEADS, Q_PER_KV)[None, None, :, :, None, None]
    m = jnp.maximum(jnp.max(s, axis=-1, keepdims=True), sink)
    p = jnp.exp(s - m)
    denom = jnp.sum(p, axis=-1, keepdims=True) + jnp.exp(sink - m)
    return jnp.einsum('bnkgqs,bnskd->bnqkgd', (p / denom).astype(v.dtype), v)


def attn_prompt(q, k, v, sinks):
    Bn, L = q.shape[:2]
    nb = L // WINDOW
    qb = q.reshape(Bn, nb, WINDOW, N_KV_HEADS, Q_PER_KV, HEAD_DIM)
    kb = k.reshape(Bn, nb, WINDOW, N_KV_HEADS, HEAD_DIM)
    vb = v.reshape(Bn, nb, WINDOW, N_KV_HEADS, HEAD_DIM)
    prev = lambda t: jnp.concatenate([jnp.zeros_like(t[:, :1]), t[:, :-1]], axis=1)
    kk = jnp.concatenate([prev(kb), kb], axis=2)
    vv = jnp.concatenate([prev(vb), vb], axis=2)
    blk = jnp.arange(nb)[:, None] * WINDOW
    qpos = blk + jnp.arange(WINDOW)[None, :]
    kpos = blk - WINDOW + jnp.arange(2 * WINDOW)[None, :]
    diff = qpos[:, :, None] - kpos[:, None, :]
    mask = (diff >= 0) & (diff < WINDOW) & (kpos[:, None, :] >= 0)
    o = sink_attention(qb, kk, vv, mask, sinks)
    return o.reshape(Bn, L, D_ATTN)


def attn_sample(q, k, v, k_cache, v_cache, sinks):
    Bn, L = q.shape[:2]
    kk = jnp.concatenate([k_cache.astype(k.dtype), k], axis=1)
    vv = jnp.concatenate([v_cache.astype(v.dtype), v], axis=1)
    qpos = PAST_LEN + jnp.arange(L)
    kpos = PAST_LEN - WINDOW + jnp.arange(WINDOW + L)
    diff = qpos[:, None] - kpos[None, :]
    mask = (diff >= 0) & (diff < WINDOW)
    o = sink_attention(q.reshape(Bn, 1, L, N_KV_HEADS, Q_PER_KV, HEAD_DIM),
                       kk[:, None], vv[:, None], mask[None], sinks)
    return o.reshape(Bn, L, D_ATTN), kk[:, -WINDOW:], vv[:, -WINDOW:]


def spatial_mix(v, w_s, b_s):
    Bn, L, _ = v.shape
    c = min(L, SGU_CHUNK)
    vc = v.reshape(Bn, L // c, c, N_SGU_HEADS, SGU_HEAD_DIM)
    w = jnp.tril(w_s[:, :c, :c])
    out = jnp.einsum('hts,bnshc->bnthc', w, vc) + b_s[:, :c].T[None, None, :, :, None]
    return out.reshape(Bn, L, D_SGU)


def ssm_discretize(lam_re, lam_im, log_dt, b_re, b_im):
    lr = lam_re.astype(jnp.float32)
    li = lam_im.astype(jnp.float32)
    dt = jnp.exp(log_dt.astype(jnp.float32))[:, None]
    mag = jnp.exp(lr * dt)
    a_re = mag * jnp.cos(li * dt)
    a_im = mag * jnp.sin(li * dt)
    nr, ni = a_re - 1.0, a_im
    den = lr * lr + li * li
    c_re = (nr * lr + ni * li) / den
    c_im = (ni * lr - nr * li) / den
    br = b_re.astype(jnp.float32)
    bi = b_im.astype(jnp.float32)
    bb_re = c_re[..., None] * br - c_im[..., None] * bi
    bb_im = c_re[..., None] * bi + c_im[..., None] * br
    return a_re, a_im, bb_re, bb_im


def ssm_combine(e1, e2):
    a1r, a1i, b1r, b1i = e1
    a2r, a2i, b2r, b2i = e2
    return (a1r * a2r - a1i * a2i,
            a1r * a2i + a1i * a2r,
            a2r * b1r - a2i * b1i + b2r,
            a2r * b1i + a2i * b1r + b2i)


def ssm_scan(u, lam_re, lam_im, log_dt, b_re, b_im, c_re, c_im, d, h0_re=None, h0_im=None):
    Bn, L, _ = u.shape
    ug = u.astype(jnp.float32).reshape(Bn, L, N_SSM_GROUPS, SSM_GROUP)
    a_re, a_im, bb_re, bb_im = ssm_discretize(lam_re, lam_im, log_dt, b_re, b_im)
    x_re = jnp.einsum('gpc,blgc->blgp', bb_re, ug)
    x_im = jnp.einsum('gpc,blgc->blgp', bb_im, ug)
    if h0_re is not None:
        hr = h0_re.astype(jnp.float32)
        hi = h0_im.astype(jnp.float32)
        x_re = x_re.at[:, 0].add(a_re * hr - a_im * hi)
        x_im = x_im.at[:, 0].add(a_re * hi + a_im * hr)
    ar = jnp.broadcast_to(a_re, x_re.shape)
    ai = jnp.broadcast_to(a_im, x_im.shape)
    _, _, h_re, h_im = lax.associative_scan(ssm_combine, (ar, ai, x_re, x_im), axis=1)
    y = (jnp.einsum('gcp,blgp->blgc', c_re.astype(jnp.float32), h_re)
         - jnp.einsum('gcp,blgp->blgc', c_im.astype(jnp.float32), h_im)
         + d.astype(jnp.float32) * ug)
    return y.reshape(Bn, L, D_SSM).astype(u.dtype), h_re[:, -1], h_im[:, -1]


def token_mixers(h, pos, w_in, sinks, ln_g, ln_b, sgu_w, sgu_b, lam_re, lam_im, log_dt,
                 b_re, b_im, c_re, c_im, d, glu_w, glu_b, mix_g, w_out,
                 k_cache=None, v_cache=None, h0_re=None, h0_im=None):
    Bn, L, _ = h.shape
    z = h @ w_in
    o1 = D_ATTN
    o2 = o1 + D_KV
    o3 = o2 + D_KV
    o4 = o3 + D_SGU
    o5 = o4 + D_SGU
    q = rope(z[..., :o1].reshape(Bn, L, N_Q_HEADS, HEAD_DIM), pos)
    k = rope(z[..., o1:o2].reshape(Bn, L, N_KV_HEADS, HEAD_DIM), pos)
    v = z[..., o2:o3].reshape(Bn, L, N_KV_HEADS, HEAD_DIM)
    if k_cache is None:
        attn = attn_prompt(q, k, v, sinks)
        k_state, v_state = k[:, -WINDOW:], v[:, -WINDOW:]
    else:
        attn, k_state, v_state = attn_sample(q, k, v, k_cache, v_cache, sinks)
    u_s = jax.nn.gelu(z[..., o3:o4], approximate=False)
    v_s = layer_norm(jax.nn.gelu(z[..., o4:o5], approximate=False), ln_g, ln_b)
    sgu = u_s * spatial_mix(v_s, sgu_w, sgu_b)
    y, h_re, h_im = ssm_scan(z[..., o5:], lam_re, lam_im, log_dt, b_re, b_im, c_re, c_im, d, h0_re, h0_im)
    g = jax.nn.gelu(y, approximate=False)
    ssm = g * jax.nn.sigmoid(g @ glu_w + glu_b)
    mixed = jnp.concatenate([rms_scale(attn), rms_scale(sgu), rms_scale(ssm)], axis=-1)
    mixed = (mixed * mix_g.astype(jnp.float32)).astype(h.dtype)
    return mixed @ w_out, k_state, v_state, v_s, h_re, h_im


def swiglu(x, wg, wu, wd):
    return (jax.nn.silu(x @ wg) * (x @ wu)) @ wd


def moe_swiglu(x, router, wg, wu, wd):
    flat = x.reshape(-1, D_MODEL)
    probs = jax.nn.softmax((flat @ router).astype(jnp.float32), axis=-1)
    top_p, top_i = lax.top_k(probs, TOP_K)
    top_p = top_p / jnp.sum(top_p, axis=-1, keepdims=True)
    comb = jnp.sum(jax.nn.one_hot(top_i, N_EXPERTS, dtype=jnp.float32) * top_p[..., None], axis=1)
    out = jnp.zeros_like(flat)
    for e in range(N_EXPERTS):
        out = out + comb[:, e:e + 1].astype(flat.dtype) * swiglu(flat, wg[e], wu[e], wd[e])
    return out.reshape(x.shape)


def setup_inputs(seed: int = 0) -> dict:
    key = jax.random.key(seed)
    keys = list(jax.random.split(key, 40))
    f32 = jnp.float32

    def nrm(shape, scale):
        return jax.random.normal(keys.pop(), shape, f32) * scale

    inp = {}
    inp['x_prompt'] = nrm((BATCH, SEQ, D_MODEL), 1.0)
    inp['x_sample'] = nrm((DEC_BATCH, DEC_SEQ, D_MODEL), 1.0)
    inp['cache_k'] = nrm((DEPTH, DEC_BATCH, WINDOW, N_KV_HEADS, HEAD_DIM), 1.0)
    inp['cache_v'] = nrm((DEPTH, DEC_BATCH, WINDOW, N_KV_HEADS, HEAD_DIM), 1.0)
    inp['state_ssm_re'] = nrm((DEPTH, DEC_BATCH, N_SSM_GROUPS, SSM_STATE), 0.1)
    inp['state_ssm_im'] = nrm((DEPTH, DEC_BATCH, N_SSM_GROUPS, SSM_STATE), 0.1)
    inp['norm1_g'] = 1.0 + nrm((DEPTH, D_MODEL), 0.01)
    inp['w_in'] = nrm((DEPTH, D_MODEL, D_IN_PROJ), D_MODEL ** -0.5)
    inp['attn_sinks'] = nrm((DEPTH, N_Q_HEADS), 1.0)
    inp['sgu_ln_g'] = 1.0 + nrm((DEPTH, D_SGU), 0.01)
    inp['sgu_ln_b'] = nrm((DEPTH, D_SGU), 0.01)
    inp['sgu_w'] = nrm((DEPTH, N_SGU_HEADS, SGU_CHUNK, SGU_CHUNK), SGU_CHUNK ** -0.5)
    inp['sgu_b'] = 1.0 + nrm((DEPTH, N_SGU_HEADS, SGU_CHUNK), 0.01)
    inp['ssm_lambda_re'] = -0.5 + nrm((DEPTH, N_SSM_GROUPS, SSM_STATE), 0.01)
    inp['ssm_lambda_im'] = (jnp.pi * jnp.arange(SSM_STATE, dtype=f32))[None, None, :] + nrm((DEPTH, N_SSM_GROUPS, SSM_STATE), 0.01)
    inp['ssm_log_dt'] = jax.random.uniform(keys.pop(), (DEPTH, N_SSM_GROUPS), f32, math.log(0.001), math.log(0.1))
    inp['ssm_b_re'] = nrm((DEPTH, N_SSM_GROUPS, SSM_STATE, SSM_GROUP), (2 * SSM_GROUP) ** -0.5)
    inp['ssm_b_im'] = nrm((DEPTH, N_SSM_GROUPS, SSM_STATE, SSM_GROUP), (2 * SSM_GROUP) ** -0.5)
    inp['ssm_c_re'] = nrm((DEPTH, N_SSM_GROUPS, SSM_GROUP, SSM_STATE), SSM_STATE ** -0.5)
    inp['ssm_c_im'] = nrm((DEPTH, N_SSM_GROUPS, SSM_GROUP, SSM_STATE), SSM_STATE ** -0.5)
    inp['ssm_d'] = nrm((DEPTH, N_SSM_GROUPS, SSM_GROUP), 1.0)
    inp['ssm_glu_w'] = nrm((DEPTH, D_SSM, D_SSM), D_SSM ** -0.5)
    inp['ssm_glu_b'] = nrm((DEPTH, D_SSM), 0.01)
    inp['mix_norm_g'] = 1.0 + nrm((DEPTH, D_MIX), 0.01)
    inp['w_out'] = nrm((DEPTH, D_MIX, D_MODEL), D_MIX ** -0.5)
    inp['norm2_g'] = 1.0 + nrm((DEPTH, D_MODEL), 0.01)
    inp['ffn_w_gate'] = nrm((N_DENSE, D_MODEL, D_FF), D_MODEL ** -0.5)
    inp['ffn_w_up'] = nrm((N_DENSE, D_MODEL, D_FF), D_MODEL ** -0.5)
    inp['ffn_w_down'] = nrm((N_DENSE, D_FF, D_MODEL), D_FF ** -0.5)
    inp['moe_router'] = nrm((N_MOE, D_MODEL, N_EXPERTS), D_MODEL ** -0.5)
    inp['moe_w_gate'] = nrm((N_MOE, N_EXPERTS, D_MODEL, D_FF_EXPERT), D_MODEL ** -0.5)
    inp['moe_w_up'] = nrm((N_MOE, N_EXPERTS, D_MODEL, D_FF_EXPERT), D_MODEL ** -0.5)
    inp['moe_w_down'] = nrm((N_MOE, N_EXPERTS, D_FF_EXPERT, D_MODEL), D_FF_EXPERT ** -0.5)
    inp['final_norm_g'] = 1.0 + nrm((D_MODEL,), 0.01)
    return inp


def reference(x_prompt, x_sample, cache_k, cache_v, state_ssm_re, state_ssm_im,
              norm1_g, w_in, attn_sinks, sgu_ln_g, sgu_ln_b, sgu_w, sgu_b,
              ssm_lambda_re, ssm_lambda_im, ssm_log_dt, ssm_b_re, ssm_b_im,
              ssm_c_re, ssm_c_im, ssm_d, ssm_glu_w, ssm_glu_b, mix_norm_g, w_out,
              norm2_g, ffn_w_gate, ffn_w_up, ffn_w_down,
              moe_router, moe_w_gate, moe_w_up, moe_w_down, final_norm_g):
    pos_p = jnp.arange(x_prompt.shape[1], dtype=jnp.int32)
    pos_s = PAST_LEN + jnp.arange(x_sample.shape[1], dtype=jnp.int32)
    xp, xs = x_prompt, x_sample
    kp, vp, hrp, hip, ks, vs, hrs, his, sgs = ([] for _ in range(9))
    for l in range(DEPTH):
        mix_w = (w_in[l], attn_sinks[l], sgu_ln_g[l], sgu_ln_b[l], sgu_w[l], sgu_b[l],
                 ssm_lambda_re[l], ssm_lambda_im[l], ssm_log_dt[l], ssm_b_re[l], ssm_b_im[l],
                 ssm_c_re[l], ssm_c_im[l], ssm_d[l], ssm_glu_w[l], ssm_glu_b[l],
                 mix_norm_g[l], w_out[l])
        mp, k_p, v_p, _, hr_p, hi_p = token_mixers(rms_norm(xp, norm1_g[l]), pos_p, *mix_w)
        ms, k_s, v_s, sv_s, hr_s, hi_s = token_mixers(
            rms_norm(xs, norm1_g[l]), pos_s, *mix_w,
            k_cache=cache_k[l], v_cache=cache_v[l],
            h0_re=state_ssm_re[l], h0_im=state_ssm_im[l])
        xp = xp + mp
        xs = xs + ms
        kp.append(k_p); vp.append(v_p); hrp.append(hr_p); hip.append(hi_p)
        ks.append(k_s); vs.append(v_s); hrs.append(hr_s); his.append(hi_s); sgs.append(sv_s)
        i = l // 2
        hp2 = rms_norm(xp, norm2_g[l])
        hs2 = rms_norm(xs, norm2_g[l])
        if l % 2 == 0:
            xp = xp + swiglu(hp2, ffn_w_gate[i], ffn_w_up[i], ffn_w_down[i])
            xs = xs + swiglu(hs2, ffn_w_gate[i], ffn_w_up[i], ffn_w_down[i])
        else:
            xp = xp + moe_swiglu(hp2, moe_router[i], moe_w_gate[i], moe_w_up[i], moe_w_down[i])
            xs = xs + moe_swiglu(hs2, moe_router[i], moe_w_gate[i], moe_w_up[i], moe_w_down[i])
    y_prompt = rms_norm(xp, final_norm_g)
    y_sample = rms_norm(xs, final_norm_g)
    new_k_prompt = jnp.stack(kp)
    new_v_prompt = jnp.stack(vp)
    new_ssm_re_prompt = jnp.stack(hrp)
    new_ssm_im_prompt = jnp.stack(hip)
    new_k_sample = jnp.stack(ks)
    new_v_sample = jnp.stack(vs)
    new_ssm_re_sample = jnp.stack(hrs)
    new_ssm_im_sample = jnp.stack(his)
    new_sgu_v_sample = jnp.stack(sgs)
    return (y_prompt, y_sample, new_k_prompt, new_v_prompt, new_ssm_re_prompt, new_ssm_im_prompt,
            new_k_sample, new_v_sample, new_ssm_re_sample, new_ssm_im_sample, new_sgu_v_sample)
```

```python
import functools
import math

import jax
import jax.numpy as jnp
from jax import lax
from jax.experimental import pallas as pl
from jax.experimental.pallas import tpu as pltpu

F32 = jnp.float32
BF16 = jnp.bfloat16

D_MODEL = 2048
BATCH = 4
SEQ = 2048
DEPTH = 4
DEC_BATCH = 32
DEC_SEQ = 8
PAST_LEN = 16384
HEAD_DIM = 64
D_ATTN = 1024
N_Q_HEADS = 16
N_KV_HEADS = 4
Q_PER_KV = 4
D_KV = 256
WINDOW = 128
ROPE_THETA = 10000.0
D_SGU = 512
N_SGU_HEADS = 4
SGU_HEAD_DIM = 128
D_SSM = 512
SSM_GROUP = 16
N_SSM_GROUPS = 32
SSM_STATE = 64
N_STATE = N_SSM_GROUPS * SSM_STATE
D_IN_PROJ = 3072
N_EXPERTS = 8
NORM_EPS = 1e-6
NEG_INF = -1e30
SQRT_HALF = math.sqrt(0.5)

CHUNK = 128
N_CHUNKS = SEQ // CHUNK
T_PROMPT = BATCH * SEQ
T_SAMPLE = DEC_BATCH * DEC_SEQ
T_ALL = T_PROMPT + T_SAMPLE

VMEM_LIMIT_BYTES = 56 * 1024 * 1024

ROW_TILE = 768
MOE_ROW_TILE = 1024
MOE_TILES = (2 * T_ALL) // MOE_ROW_TILE + N_EXPERTS
FF_TILE = 256
GATHER_TILE = 256


def _params(*sem):
    return pltpu.CompilerParams(dimension_semantics=sem, vmem_limit_bytes=VMEM_LIMIT_BYTES)


def _rms_scale(x):
    return x * lax.rsqrt(jnp.mean(x * x, axis=-1, keepdims=True) + NORM_EPS)


def _gelu(x):
    return 0.5 * x * (1.0 + lax.erf(x * SQRT_HALF))


def _dot(a, b):
    return jnp.dot(a, b, preferred_element_type=F32)


def _norm_matmul_kernel(x_ref, g_ref, w_ref, o_ref, h_ref):
    @pl.when(pl.program_id(1) == 0)
    def _():
        h_ref[...] = (_rms_scale(x_ref[...]) * g_ref[...]).astype(BF16)

    o_ref[...] = _dot(h_ref[...], w_ref[...].astype(BF16))


def norm_matmul(x, g_all, w_all, layer, *, tm, tn):
    t, k = x.shape
    n = w_all.shape[-1]
    return pl.pallas_call(
        _norm_matmul_kernel,
        grid=(t // tm, n // tn),
        in_specs=[
            pl.BlockSpec((tm, k), lambda i, j: (i, 0)),
            pl.BlockSpec((None, 1, k), lambda i, j: (layer, 0, 0)),
            pl.BlockSpec((None, k, tn), lambda i, j: (layer, 0, j)),
        ],
        out_specs=pl.BlockSpec((tm, tn), lambda i, j: (i, j)),
        out_shape=jax.ShapeDtypeStruct((t, n), F32),
        scratch_shapes=[pltpu.VMEM((tm, k), BF16)],
        compiler_params=_params("arbitrary", "arbitrary"),
        name="norm_matmul",
    )(x, g_all, w_all)


def _rope(x, cos, sin_signed):
    lane = lax.broadcasted_iota(jnp.int32, (x.shape[0], 128), 1)
    first_half = (lane % HEAD_DIM) < (HEAD_DIM // 2)
    outs = []
    for j in range(x.shape[1] // 128):
        xc = x[:, j * 128:(j + 1) * 128]
        swapped = jnp.where(first_half, pltpu.roll(xc, 96, 1), pltpu.roll(xc, 32, 1))
        outs.append(xc * cos + swapped * sin_signed)
    return jnp.concatenate(outs, axis=1) if len(outs) > 1 else outs[0]


def _attend(q, kk, vv, mask, sink_ref):
    r = q.shape[0]
    mask4 = jnp.concatenate([mask] * Q_PER_KV, axis=0)
    outs = []
    for kh in range(N_KV_HEADS):
        k = kk[:, kh * HEAD_DIM:(kh + 1) * HEAD_DIM].astype(BF16)
        v = vv[:, kh * HEAD_DIM:(kh + 1) * HEAD_DIM].astype(BF16)
        heads = [kh * Q_PER_KV + i for i in range(Q_PER_KV)]
        qs = jnp.concatenate([q[:, h * HEAD_DIM:(h + 1) * HEAD_DIM] for h in heads], axis=0).astype(BF16)
        s = lax.dot_general(qs, k, (((1,), (1,)), ((), ())), preferred_element_type=F32)
        s = s * (HEAD_DIM ** -0.5)
        s = jnp.where(mask4, s, NEG_INF)
        sink = jnp.concatenate([jnp.full((r, 1), sink_ref[h], F32) for h in heads], axis=0)
        m = jnp.maximum(jnp.max(s, axis=-1, keepdims=True), sink)
        p = jnp.exp(s - m)
        denom = jnp.sum(p, axis=-1, keepdims=True) + jnp.exp(sink - m)
        o = _dot((p / denom).astype(BF16), v)
        outs.extend(o[i * r:(i + 1) * r] for i in range(Q_PER_KV))
    return jnp.concatenate(outs, axis=1)


def _attn_prompt_kernel(sink_ref, q_ref, kc_ref, kp_ref, vc_ref, vp_ref, cosc_ref, sinc_ref,
                        cosp_ref, sinp_ref, g_ref, init_ref, o_ref, ks_ref, vs_ref):
    del init_ref
    c = pl.program_id(1)
    q = _rope(q_ref[...], cosc_ref[...], sinc_ref[...])
    kc = _rope(kc_ref[...], cosc_ref[...], sinc_ref[...])
    kp = _rope(kp_ref[...], cosp_ref[...], sinp_ref[...])
    kk = jnp.concatenate([kp, kc], axis=0)
    vv = jnp.concatenate([vp_ref[...], vc_ref[...]], axis=0)
    row = lax.broadcasted_iota(jnp.int32, (CHUNK, 2 * CHUNK), 0)
    col = lax.broadcasted_iota(jnp.int32, (CHUNK, 2 * CHUNK), 1)
    mask = (col > row) & (col <= row + WINDOW) & ((col >= CHUNK) | (c > 0))
    attn = _attend(q, kk, vv, mask, sink_ref)
    o_ref[...] = (_rms_scale(attn) * g_ref[...]).astype(BF16)
    ks_ref[...] = kc
    vs_ref[...] = vc_ref[...]


def attn_prompt(z, sinks, cos_t, sin_t, mix_g_all, layer):
    blk = lambda b, c: c * BATCH + b
    prev = lambda b, c: jnp.maximum(c - 1, 0) * BATCH + b
    kcol, vcol = D_ATTN // D_KV, D_ATTN // D_KV + 1
    return pl.pallas_call(
        _attn_prompt_kernel,
        grid=(BATCH, N_CHUNKS),
        in_specs=[
            pl.BlockSpec(memory_space=pltpu.SMEM),
            pl.BlockSpec((CHUNK, D_ATTN), lambda b, c: (blk(b, c), 0)),
            pl.BlockSpec((CHUNK, D_KV), lambda b, c: (blk(b, c), kcol)),
            pl.BlockSpec((CHUNK, D_KV), lambda b, c: (prev(b, c), kcol)),
            pl.BlockSpec((CHUNK, D_KV), lambda b, c: (blk(b, c), vcol)),
            pl.BlockSpec((CHUNK, D_KV), lambda b, c: (prev(b, c), vcol)),
            pl.BlockSpec((CHUNK, 128), lambda b, c: (c, 0)),
            pl.BlockSpec((CHUNK, 128), lambda b, c: (c, 0)),
            pl.BlockSpec((CHUNK, 128), lambda b, c: (jnp.maximum(c - 1, 0), 0)),
            pl.BlockSpec((CHUNK, 128), lambda b, c: (jnp.maximum(c - 1, 0), 0)),
            pl.BlockSpec((None, 1, D_ATTN), lambda b, c: (layer, 0, 0)),
            pl.BlockSpec(memory_space=pl.ANY),
        ],
        out_specs=[
            pl.BlockSpec((CHUNK, D_ATTN), lambda b, c: (blk(b, c), 0)),
            pl.BlockSpec((None, CHUNK, D_KV), lambda b, c: (b, 0, 0)),
            pl.BlockSpec((None, CHUNK, D_KV), lambda b, c: (b, 0, 0)),
        ],
        out_shape=[
            jax.ShapeDtypeStruct((T_ALL, D_ATTN), BF16),
            jax.ShapeDtypeStruct((BATCH, CHUNK, D_KV), F32),
            jax.ShapeDtypeStruct((BATCH, CHUNK, D_KV), F32),
        ],
        input_output_aliases={11: 0},
        compiler_params=_params("arbitrary", "arbitrary"),
        name="attn_prompt",
    )(sinks, z, z, z, z, z, cos_t, sin_t, cos_t, sin_t, mix_g_all,
      jnp.zeros((T_ALL, D_ATTN), BF16))


SAMPLE_SEQS_PER_STEP = 4


def _attn_sample_kernel(sink_ref, q_ref, kn_ref, vn_ref, ck_ref, cv_ref, cos_ref, sin_ref, g_ref,
                        prev_ref, o_ref, ks_ref, vs_ref):
    del prev_ref
    row = lax.broadcasted_iota(jnp.int32, (DEC_SEQ, WINDOW + DEC_SEQ), 0)
    col = lax.broadcasted_iota(jnp.int32, (DEC_SEQ, WINDOW + DEC_SEQ), 1)
    mask = (col > row) & (col <= row + WINDOW)
    normed = []
    for s in range(SAMPLE_SEQS_PER_STEP):
        rows = slice(s * DEC_SEQ, (s + 1) * DEC_SEQ)
        q = _rope(q_ref[rows, :], cos_ref[...], sin_ref[...])
        kn = _rope(kn_ref[rows, :], cos_ref[...], sin_ref[...])
        vn = vn_ref[rows, :]
        kk = jnp.concatenate([ck_ref[s], kn], axis=0)
        vv = jnp.concatenate([cv_ref[s], vn], axis=0)
        attn = _attend(q, kk, vv, mask, sink_ref)
        normed.append(_rms_scale(attn) * g_ref[...])
        ks_ref[s] = kk[DEC_SEQ:]
        vs_ref[s] = vv[DEC_SEQ:]
    o_ref[...] = jnp.concatenate(normed, axis=0).astype(BF16)


def attn_sample(z, attn_n, cache_k, cache_v, sinks, cos_s, sin_s, mix_g_all, layer):
    nb = SAMPLE_SEQS_PER_STEP
    rows = nb * DEC_SEQ
    base = T_PROMPT // rows
    kcol, vcol = D_ATTN // D_KV, D_ATTN // D_KV + 1
    return pl.pallas_call(
        _attn_sample_kernel,
        grid=(DEC_BATCH // nb,),
        in_specs=[
            pl.BlockSpec(memory_space=pltpu.SMEM),
            pl.BlockSpec((rows, D_ATTN), lambda i: (base + i, 0)),
            pl.BlockSpec((rows, D_KV), lambda i: (base + i, kcol)),
            pl.BlockSpec((rows, D_KV), lambda i: (base + i, vcol)),
            pl.BlockSpec((None, nb, WINDOW, D_KV), lambda i: (layer, i, 0, 0)),
            pl.BlockSpec((None, nb, WINDOW, D_KV), lambda i: (layer, i, 0, 0)),
            pl.BlockSpec((DEC_SEQ, 128), lambda i: (0, 0)),
            pl.BlockSpec((DEC_SEQ, 128), lambda i: (0, 0)),
            pl.BlockSpec((None, 1, D_ATTN), lambda i: (layer, 0, 0)),
            pl.BlockSpec(memory_space=pl.ANY),
        ],
        out_specs=[
            pl.BlockSpec((rows, D_ATTN), lambda i: (base + i, 0)),
            pl.BlockSpec((nb, WINDOW, D_KV), lambda i: (i, 0, 0)),
            pl.BlockSpec((nb, WINDOW, D_KV), lambda i: (i, 0, 0)),
        ],
        out_shape=[
            jax.ShapeDtypeStruct((T_ALL, D_ATTN), BF16),
            jax.ShapeDtypeStruct((DEC_BATCH, WINDOW, D_KV), F32),
            jax.ShapeDtypeStruct((DEC_BATCH, WINDOW, D_KV), F32),
        ],
        input_output_aliases={9: 0},
        compiler_params=_params("arbitrary"),
        name="attn_sample",
    )(sinks, z, z, z, cache_k, cache_v, cos_s, sin_s, mix_g_all, attn_n)


def _sgu_kernel(u_ref, v_ref, w_ref, bias_ref, lng_ref, lnb_ref, g_ref, o_ref, sv_ref):
    u = _gelu(u_ref[...])
    v = _gelu(v_ref[...])
    vc = v - jnp.mean(v, axis=-1, keepdims=True)
    v = vc * lax.rsqrt(jnp.mean(vc * vc, axis=-1, keepdims=True) + NORM_EPS)
    v = v * lng_ref[...] + lnb_ref[...]
    sv_ref[...] = v
    row = lax.broadcasted_iota(jnp.int32, (CHUNK, CHUNK), 0)
    col = lax.broadcasted_iota(jnp.int32, (CHUNK, CHUNK), 1)
    causal = col <= row
    vb = v.astype(BF16)
    mixed = []
    for h in range(N_SGU_HEADS):
        w = jnp.where(causal, w_ref[h], 0.0).astype(BF16)
        mixed.append(_dot(w, vb[:, h * SGU_HEAD_DIM:(h + 1) * SGU_HEAD_DIM]))
    sgu = u * (jnp.concatenate(mixed, axis=1) + bias_ref[...])
    o_ref[...] = (_rms_scale(sgu) * g_ref[...]).astype(BF16)


def sgu(z, w_mix, bias_mix, ln_g_all, ln_b_all, mix_g_all, layer):
    n_prompt = T_PROMPT // CHUNK
    kind = lambda i: jnp.where(i >= n_prompt, 1, 0)
    ucol, vcol = (D_ATTN + 2 * D_KV) // D_SGU, (D_ATTN + 2 * D_KV) // D_SGU + 1
    gcol = D_ATTN // D_SGU
    return pl.pallas_call(
        _sgu_kernel,
        grid=(T_ALL // CHUNK,),
        in_specs=[
            pl.BlockSpec((CHUNK, D_SGU), lambda i: (i, ucol)),
            pl.BlockSpec((CHUNK, D_SGU), lambda i: (i, vcol)),
            pl.BlockSpec((None, None, N_SGU_HEADS, CHUNK, CHUNK), lambda i: (layer, kind(i), 0, 0, 0)),
            pl.BlockSpec((None, None, CHUNK, D_SGU), lambda i: (layer, kind(i), 0, 0)),
            pl.BlockSpec((None, 1, D_SGU), lambda i: (layer, 0, 0)),
            pl.BlockSpec((None, 1, D_SGU), lambda i: (layer, 0, 0)),
            pl.BlockSpec((None, 1, D_SGU), lambda i: (layer, 0, gcol)),
        ],
        out_specs=[
            pl.BlockSpec((CHUNK, D_SGU), lambda i: (i, 0)),
            pl.BlockSpec((CHUNK, D_SGU), lambda i: (jnp.maximum(i - n_prompt, 0), 0)),
        ],
        out_shape=[
            jax.ShapeDtypeStruct((T_ALL, D_SGU), BF16),
            jax.ShapeDtypeStruct((T_SAMPLE, D_SGU), F32),
        ],
        compiler_params=_params("arbitrary"),
        name="sgu",
    )(z, z, w_mix, bias_mix, ln_g_all, ln_b_all, mix_g_all)


def _ssm_disc_kernel(lr_ref, li_ref, ldt_ref, br_ref, bi_ref, ar_ref, ai_ref, bbr_ref, bbi_ref):
    lr = lr_ref[...]
    li = li_ref[...]
    dt = jnp.exp(ldt_ref[...])
    mag = jnp.exp(lr * dt)
    a_re = mag * jnp.cos(li * dt)
    a_im = mag * jnp.sin(li * dt)
    nr, ni = a_re - 1.0, a_im
    den = lr * lr + li * li
    c_re = (nr * lr + ni * li) / den
    c_im = (ni * lr - nr * li) / den
    ar_ref[...] = a_re
    ai_ref[...] = a_im
    bbr_ref[...] = c_re * br_ref[...] - c_im * bi_ref[...]
    bbi_ref[...] = c_re * bi_ref[...] + c_im * br_ref[...]


def ssm_discretize(lam_re, lam_im, log_dt, b_re_t, b_im_t):
    d, g, p = lam_re.shape
    c = b_re_t.shape[2]
    return pl.pallas_call(
        _ssm_disc_kernel,
        out_shape=[
            jax.ShapeDtypeStruct((d, g, 1, p), F32),
            jax.ShapeDtypeStruct((d, g, 1, p), F32),
            jax.ShapeDtypeStruct((d, g, c, p), F32),
            jax.ShapeDtypeStruct((d, g, c, p), F32),
        ],
        compiler_params=pltpu.CompilerParams(vmem_limit_bytes=VMEM_LIMIT_BYTES),
        name="ssm_discretize",
    )(lam_re.reshape(d, g, 1, p), lam_im.reshape(d, g, 1, p), log_dt.reshape(d, g, 1, 1),
      b_re_t, b_im_t)


SCAN_COLS = 512
SCAN_BLOCKS = SCAN_COLS // 128


def _ssm_kernel(u_ref, bre_ref, bim_ref, cre_ref, cim_ref, are_ref, aim_ref, d_ref, gw_ref, gb_ref,
                g_ref, h0r_ref, h0i_ref, prev_ref, o_ref, hr_ref, hi_ref, xr_ref, xi_ref, *, ns, lc):
    del prev_ref
    half_u = D_SSM // 2
    half_s = N_STATE // 2

    @pl.when(pl.program_id(0) == 0)
    def _():
        hr_ref[...] = h0r_ref[...]
        hi_ref[...] = h0i_ref[...]

    u = u_ref[...]
    ub = u.astype(BF16)
    blocks_per_half = half_s // 128
    for hf in range(2):
        uh = ub[:, hf * half_u:(hf + 1) * half_u]
        x_re = _dot(uh, bre_ref[hf].astype(BF16))
        x_im = _dot(uh, bim_ref[hf].astype(BF16))
        for j in range(blocks_per_half):
            xr_ref[hf * blocks_per_half + j] = x_re[:, j * 128:(j + 1) * 128]
            xi_ref[hf * blocks_per_half + j] = x_im[:, j * 128:(j + 1) * 128]

    for cc in range(N_STATE // SCAN_COLS):
        blocks = range(cc * SCAN_BLOCKS, (cc + 1) * SCAN_BLOCKS)
        a_re = [jnp.broadcast_to(are_ref[:, j * 128:(j + 1) * 128], (ns, 128)) for j in blocks]
        a_im = [jnp.broadcast_to(aim_ref[:, j * 128:(j + 1) * 128], (ns, 128)) for j in blocks]

        def step(t, carry):
            rows = pl.ds(t, ns, stride=lc)
            new = []
            for n, j in enumerate(blocks):
                h_re, h_im = carry[n]
                n_re = a_re[n] * h_re - a_im[n] * h_im + xr_ref[j, rows, :]
                n_im = a_re[n] * h_im + a_im[n] * h_re + xi_ref[j, rows, :]
                xr_ref[j, rows, :] = n_re
                xi_ref[j, rows, :] = n_im
                new.append((n_re, n_im))
            return tuple(new)

        init = tuple((hr_ref[:, j * 128:(j + 1) * 128], hi_ref[:, j * 128:(j + 1) * 128]) for j in blocks)
        final = lax.fori_loop(0, lc, step, init)
        for n, j in enumerate(blocks):
            hr_ref[:, j * 128:(j + 1) * 128] = final[n][0]
            hi_ref[:, j * 128:(j + 1) * 128] = final[n][1]

    ys = []
    for hf in range(2):
        js = range(hf * blocks_per_half, (hf + 1) * blocks_per_half)
        h_re = jnp.concatenate([xr_ref[j].astype(BF16) for j in js], axis=1)
        h_im = jnp.concatenate([xi_ref[j].astype(BF16) for j in js], axis=1)
        ys.append(_dot(h_re, cre_ref[hf].astype(BF16)) - _dot(h_im, cim_ref[hf].astype(BF16)))
    y = jnp.concatenate(ys, axis=1) + d_ref[...] * u
    g = _gelu(y)
    gate = _dot(g.astype(BF16), gw_ref[...].astype(BF16)) + gb_ref[...]
    ssm = g * jax.nn.sigmoid(gate)
    o_ref[...] = (_rms_scale(ssm) * g_ref[...]).astype(BF16)


def ssm(z, prev, h0_re, h0_im, bre, bim, cre, cim, a_re, a_im, d_all, glu_w_all, glu_b_all,
        mix_g_all, layer, *, ns, lc, first_block, n_blocks):
    rows = ns * lc
    zcol = (D_ATTN + 2 * D_KV + 2 * D_SGU) // D_SSM
    gcol = (D_ATTN + D_SGU) // D_SSM
    full = lambda shape: pl.BlockSpec(shape, lambda i: (0,) * len(shape))
    lay3 = lambda shape: pl.BlockSpec((None,) + shape, lambda i: (layer,) + (0,) * len(shape))
    in_specs = [
        pl.BlockSpec((rows, D_SSM), lambda i: (first_block + i, zcol)),
        lay3((2, D_SSM // 2, N_STATE // 2)),
        lay3((2, D_SSM // 2, N_STATE // 2)),
        lay3((2, N_STATE // 2, D_SSM // 2)),
        lay3((2, N_STATE // 2, D_SSM // 2)),
        lay3((1, N_STATE)),
        lay3((1, N_STATE)),
        lay3((1, D_SSM)),
        lay3((D_SSM, D_SSM)),
        lay3((1, D_SSM)),
        pl.BlockSpec((None, 1, D_SSM), lambda i: (layer, 0, gcol)),
        full((ns, N_STATE)),
        full((ns, N_STATE)),
        pl.BlockSpec(memory_space=pl.ANY),
    ]
    args = [z, bre, bim, cre, cim, a_re, a_im, d_all, glu_w_all, glu_b_all, mix_g_all, h0_re, h0_im,
            prev]
    return pl.pallas_call(
        functools.partial(_ssm_kernel, ns=ns, lc=lc),
        grid=(n_blocks,),
        in_specs=in_specs,
        out_specs=[
            pl.BlockSpec((rows, D_SSM), lambda i: (first_block + i, 0)),
            full((ns, N_STATE)),
            full((ns, N_STATE)),
        ],
        out_shape=[
            jax.ShapeDtypeStruct((T_ALL, D_SSM), BF16),
            jax.ShapeDtypeStruct((ns, N_STATE), F32),
            jax.ShapeDtypeStruct((ns, N_STATE), F32),
        ],
        scratch_shapes=[pltpu.VMEM((N_STATE // 128, rows, 128), F32),
                        pltpu.VMEM((N_STATE // 128, rows, 128), F32)],
        input_output_aliases={13: 0},
        compiler_params=_params("arbitrary"),
        name="ssm",
    )(*args)


def _out_proj_kernel(a0_ref, a1_ref, a2_ref, w_ref, r_ref, o_ref):
    k0 = a0_ref.shape[1]
    k1 = k0 + a1_ref.shape[1]
    acc = _dot(a0_ref[...], w_ref[:k0, :].astype(BF16))
    acc += _dot(a1_ref[...], w_ref[k0:k1, :].astype(BF16))
    acc += _dot(a2_ref[...], w_ref[k1:, :].astype(BF16))
    o_ref[...] = r_ref[...] + acc


def out_proj(a0, a1, a2, w_all, resid, layer, *, tm, tn):
    t, n = resid.shape
    k = w_all.shape[1]
    return pl.pallas_call(
        _out_proj_kernel,
        grid=(t // tm, n // tn),
        in_specs=[
            pl.BlockSpec((tm, a0.shape[1]), lambda i, j: (i, 0)),
            pl.BlockSpec((tm, a1.shape[1]), lambda i, j: (i, 0)),
            pl.BlockSpec((tm, a2.shape[1]), lambda i, j: (i, 0)),
            pl.BlockSpec((None, k, tn), lambda i, j: (layer, 0, j)),
            pl.BlockSpec((tm, tn), lambda i, j: (i, j)),
        ],
        out_specs=pl.BlockSpec((tm, tn), lambda i, j: (i, j)),
        out_shape=jax.ShapeDtypeStruct((t, n), F32),
        compiler_params=_params("arbitrary", "arbitrary"),
        name="out_proj",
    )(a0, a1, a2, w_all, resid)


def _norm_kernel(x_ref, g_ref, o_ref):
    o_ref[...] = (_rms_scale(x_ref[...]) * g_ref[...]).astype(o_ref.dtype)


def norm_rows(x, g_all, layer, *, tm, dtype):
    t, k = x.shape
    return pl.pallas_call(
        _norm_kernel,
        grid=(t // tm,),
        in_specs=[pl.BlockSpec((tm, k), lambda i: (i, 0)),
                  pl.BlockSpec((None, 1, k), lambda i: (layer, 0, 0))],
        out_specs=pl.BlockSpec((tm, k), lambda i: (i, 0)),
        out_shape=jax.ShapeDtypeStruct((t, k), dtype),
        compiler_params=_params("arbitrary"),
        name="norm_rows",
    )(x, g_all)


def _router_kernel(x_ref, g_ref, w_ref, idx_ref, p_ref):
    h = (_rms_scale(x_ref[...]) * g_ref[...]).astype(BF16)
    logits = _dot(h, w_ref[...].astype(BF16))
    e = jnp.exp(logits - jnp.max(logits, axis=-1, keepdims=True))
    probs = e / jnp.sum(e, axis=-1, keepdims=True)
    lane = lax.broadcasted_iota(jnp.int32, probs.shape, 1)
    p1 = jnp.max(probs, axis=-1, keepdims=True)
    i1 = jnp.min(jnp.where(probs == p1, lane, N_EXPERTS), axis=-1, keepdims=True)
    rest = jnp.where(lane == i1, -1.0, probs)
    p2 = jnp.max(rest, axis=-1, keepdims=True)
    i2 = jnp.min(jnp.where(rest == p2, lane, N_EXPERTS), axis=-1, keepdims=True)
    tot = p1 + p2
    idx_ref[...] = jnp.where(lane == 0, i1, jnp.where(lane == 1, i2, 0))
    p_ref[...] = jnp.where(lane == 0, p1 / tot, jnp.where(lane == 1, p2 / tot, 0.0))


def router_top2(x, g_all, w_all, layer, moe_layer, *, tm):
    t, k = x.shape
    return pl.pallas_call(
        _router_kernel,
        grid=(t // tm,),
        in_specs=[pl.BlockSpec((tm, k), lambda i: (i, 0)),
                  pl.BlockSpec((None, 1, k), lambda i: (layer, 0, 0)),
                  pl.BlockSpec((None, k, N_EXPERTS), lambda i: (moe_layer, 0, 0))],
        out_specs=[pl.BlockSpec((tm, N_EXPERTS), lambda i: (i, 0)),
                   pl.BlockSpec((tm, N_EXPERTS), lambda i: (i, 0))],
        out_shape=[jax.ShapeDtypeStruct((t, N_EXPERTS), jnp.int32),
                   jax.ShapeDtypeStruct((t, N_EXPERTS), F32)],
        compiler_params=_params("arbitrary"),
        name="router_top2",
    )(x, g_all, w_all)


def _row_copy(src_hbm, row, dst, r, sem):
    return pltpu.make_async_copy(src_hbm.at[pl.ds(row, 1)], dst.at[pl.ds(r, 1)], sem)


def _gather_norm_kernel(idx_ref, x_hbm, g_ref, o_ref, buf, sem):
    tg = buf.shape[0]
    base = pl.program_id(0) * tg

    def start(r, _):
        _row_copy(x_hbm, idx_ref[base + r], buf, r, sem).start()
        return 0

    def wait(r, _):
        _row_copy(x_hbm, idx_ref[base + r], buf, r, sem).wait()
        return 0

    lax.fori_loop(0, tg, start, 0)
    lax.fori_loop(0, tg, wait, 0)
    o_ref[...] = (_rms_scale(buf[...]) * g_ref[...]).astype(BF16)


def gather_norm(x, idx, g_all, layer, *, tg):
    n = idx.shape[0]
    k = x.shape[1]
    return pl.pallas_call(
        _gather_norm_kernel,
        grid_spec=pltpu.PrefetchScalarGridSpec(
            num_scalar_prefetch=1,
            grid=(n // tg,),
            in_specs=[pl.BlockSpec(memory_space=pl.ANY),
                      pl.BlockSpec((None, 1, k), lambda i, idx: (layer, 0, 0))],
            out_specs=pl.BlockSpec((tg, k), lambda i, idx: (i, 0)),
            scratch_shapes=[pltpu.VMEM((tg, k), F32), pltpu.SemaphoreType.DMA(())],
        ),
        out_shape=jax.ShapeDtypeStruct((n, k), BF16),
        compiler_params=_params("arbitrary"),
        name="gather_norm",
    )(idx, x, g_all)


def _combine_kernel(slot_ref, o_hbm, x_ref, out_ref, buf_a, buf_b, sem):
    tc = buf_a.shape[0]
    base = pl.program_id(0) * tc

    def copies(r):
        return (_row_copy(o_hbm, slot_ref[2 * (base + r)], buf_a, r, sem),
                _row_copy(o_hbm, slot_ref[2 * (base + r) + 1], buf_b, r, sem))

    def start(r, _):
        for cp in copies(r):
            cp.start()
        return 0

    def wait(r, _):
        for cp in copies(r):
            cp.wait()
        return 0

    lax.fori_loop(0, tc, start, 0)
    lax.fori_loop(0, tc, wait, 0)
    out_ref[...] = x_ref[...] + (buf_a[...] + buf_b[...])


def moe_combine(o_sorted, slots, x, *, tc):
    t, k = x.shape
    return pl.pallas_call(
        _combine_kernel,
        grid_spec=pltpu.PrefetchScalarGridSpec(
            num_scalar_prefetch=1,
            grid=(t // tc,),
            in_specs=[pl.BlockSpec(memory_space=pl.ANY),
                      pl.BlockSpec((tc, k), lambda i, s: (i, 0))],
            out_specs=pl.BlockSpec((tc, k), lambda i, s: (i, 0)),
            scratch_shapes=[pltpu.VMEM((tc, k), F32), pltpu.VMEM((tc, k), F32),
                            pltpu.SemaphoreType.DMA(())],
        ),
        out_shape=jax.ShapeDtypeStruct((t, k), F32),
        compiler_params=_params("arbitrary"),
        name="moe_combine",
    )(slots, o_sorted, x)


def _ffn_kernel(te_ref, act_ref, x_ref, wg_ref, wu_ref, wd_ref, s_ref, *rest, nf, has_resid):
    if has_resid:
        r_ref, o_ref = rest
    else:
        (o_ref,) = rest
    i = pl.program_id(0)
    f = pl.program_id(1)
    active = act_ref[i] > 0

    @pl.when(active)
    def _():
        x = x_ref[...]
        gate = _dot(x, wg_ref[...].astype(BF16))
        up = _dot(x, wu_ref[...].astype(BF16))
        h = (gate * jax.nn.sigmoid(gate) * up).astype(BF16)
        part = _dot(h, wd_ref[...].astype(BF16))

        @pl.when(f == 0)
        def _():
            o_ref[...] = part

        @pl.when(f > 0)
        def _():
            o_ref[...] += part

    @pl.when(jnp.logical_and(active, f == nf - 1))
    def _():
        out = o_ref[...] * s_ref[...]
        if has_resid:
            out = r_ref[...] + out
        o_ref[...] = out

    @pl.when(jnp.logical_and(jnp.logical_not(active), f == nf - 1))
    def _():
        o_ref[...] = jnp.zeros_like(o_ref)


def grouped_ffn(x_rows, tile_expert, tile_active, scale, wg_all, wu_all, wd_all, lead, resid,
                *, tm, tf):
    s, k = x_rows.shape
    ff = wg_all.shape[-1]
    nf = ff // tf
    n_tiles = s // tm
    nlead = len(lead)
    fblk = lambda i, f, act: jnp.where(act[i] > 0, f, nf - 1)
    w_in_spec = pl.BlockSpec((None,) * (nlead + 1) + (k, tf),
                             lambda i, f, te, act: lead + (te[i], 0, fblk(i, f, act)))
    w_out_spec = pl.BlockSpec((None,) * (nlead + 1) + (tf, k),
                              lambda i, f, te, act: lead + (te[i], fblk(i, f, act), 0))
    row_spec = lambda width: pl.BlockSpec((tm, width), lambda i, f, te, act: (i, 0))
    in_specs = [row_spec(k), w_in_spec, w_in_spec, w_out_spec, row_spec(1)]
    args = [x_rows, wg_all, wu_all, wd_all, scale]
    if resid is not None:
        in_specs.append(row_spec(k))
        args.append(resid)
    return pl.pallas_call(
        functools.partial(_ffn_kernel, nf=nf, has_resid=resid is not None),
        grid_spec=pltpu.PrefetchScalarGridSpec(
            num_scalar_prefetch=2,
            grid=(n_tiles, nf),
            in_specs=in_specs,
            out_specs=row_spec(k),
        ),
        out_shape=jax.ShapeDtypeStruct((s, k), F32),
        compiler_params=_params("arbitrary", "arbitrary"),
        name="grouped_ffn",
    )(tile_expert, tile_active, *args)


def _moe_plan(top_i, top_p, tm, n_tiles):
    t = top_i.shape[0]
    e = top_i.reshape(-1)
    p = top_p.reshape(-1)
    onehot = (e[:, None] == jnp.arange(N_EXPERTS, dtype=jnp.int32)[None, :]).astype(jnp.int32)
    csum = jnp.cumsum(onehot, axis=0)
    rank = jnp.take_along_axis(csum, e[:, None], axis=1)[:, 0] - 1
    counts = csum[-1]
    tiles_e = (counts + tm - 1) // tm
    tile_end = jnp.cumsum(tiles_e)
    tile_start = tile_end - tiles_e
    dest = tile_start[e] * tm + rank
    src_token = jnp.zeros((n_tiles * tm,), jnp.int32).at[dest].set(
        jnp.arange(2 * t, dtype=jnp.int32) // 2, unique_indices=True)
    scale = jnp.zeros((n_tiles * tm,), F32).at[dest].set(p, unique_indices=True)
    tile_ids = jnp.arange(n_tiles, dtype=jnp.int32)
    tile_expert = jnp.sum((tile_ids[:, None] >= tile_end[None, :]).astype(jnp.int32), axis=1)
    active = (tile_ids < tile_end[-1]).astype(jnp.int32)
    last_expert = jnp.minimum(tile_expert[jnp.maximum(tile_end[-1] - 1, 0)], N_EXPERTS - 1)
    tile_expert = jnp.where(active > 0, jnp.minimum(tile_expert, N_EXPERTS - 1), last_expert)
    return src_token, scale.reshape(-1, 1), dest.astype(jnp.int32), tile_expert.astype(jnp.int32), active


def _final_norm_kernel(x_ref, g_ref, op_ref, os_ref):
    i = pl.program_id(0)
    y = _rms_scale(x_ref[...]) * g_ref[...]

    @pl.when(i < T_PROMPT // CHUNK)
    def _():
        op_ref[...] = y

    @pl.when(i >= T_PROMPT // CHUNK)
    def _():
        os_ref[...] = y


def final_norm(x, g):
    n_prompt = T_PROMPT // CHUNK
    k = x.shape[1]

    def prompt_block(i):
        j = jnp.minimum(i, n_prompt - 1)
        return ((j % BATCH) * N_CHUNKS + j // BATCH, 0)

    return pl.pallas_call(
        _final_norm_kernel,
        grid=(T_ALL // CHUNK,),
        in_specs=[pl.BlockSpec((CHUNK, k), lambda i: (i, 0)),
                  pl.BlockSpec((1, k), lambda i: (0, 0))],
        out_specs=[pl.BlockSpec((CHUNK, k), prompt_block),
                   pl.BlockSpec((CHUNK, k), lambda i: (jnp.maximum(i - n_prompt, 0), 0))],
        out_shape=[jax.ShapeDtypeStruct((T_PROMPT, k), F32),
                   jax.ShapeDtypeStruct((T_SAMPLE, k), F32)],
        compiler_params=_params("arbitrary"),
        name="final_norm",
    )(x, g.reshape(1, k))


def _rope_tables(pos):
    half = HEAD_DIM // 2
    inv_freq = jnp.power(ROPE_THETA, -jnp.arange(half, dtype=F32) / half)
    ang = pos.astype(F32)[:, None] * inv_freq[None, :]
    cos, sin = jnp.cos(ang), jnp.sin(ang)
    return jnp.tile(cos, (1, 4)), jnp.tile(jnp.concatenate([-sin, sin], axis=1), (1, 2))


def _block_diag(blocks):
    *lead, n, r, c = blocks.shape
    eye = jnp.eye(n, dtype=blocks.dtype)
    return (blocks[..., :, :, None, :] * eye[:, None, :, None]).reshape(*lead, n * r, n * c)


def kernel(x_prompt, x_sample, cache_k, cache_v, state_ssm_re, state_ssm_im, norm1_g, w_in, attn_sinks, sgu_ln_g, sgu_ln_b, sgu_w, sgu_b, ssm_lambda_re, ssm_lambda_im, ssm_log_dt, ssm_b_re, ssm_b_im, ssm_c_re, ssm_c_im, ssm_d, ssm_glu_w, ssm_glu_b, mix_norm_g, w_out, norm2_g, ffn_w_gate, ffn_w_up, ffn_w_down, moe_router, moe_w_gate, moe_w_up, moe_w_down, final_norm_g):
    d = D_MODEL
    xp = x_prompt.reshape(BATCH, N_CHUNKS, CHUNK, d).transpose(1, 0, 2, 3).reshape(T_PROMPT, d)
    x = jnp.concatenate([xp, x_sample.reshape(T_SAMPLE, d)], axis=0)

    cos_p, sin_p = _rope_tables(jnp.arange(SEQ, dtype=jnp.int32))
    cos_s, sin_s = _rope_tables(PAST_LEN + jnp.arange(DEC_SEQ, dtype=jnp.int32))

    row3 = lambda a: a.reshape(DEPTH, 1, a.shape[-1])
    norm1_g3, norm2_g3, mix_g3 = row3(norm1_g), row3(norm2_g), row3(mix_norm_g)
    ln_g3, ln_b3 = row3(sgu_ln_g), row3(sgu_ln_b)
    glu_b3 = row3(ssm_glu_b)
    ssm_d3 = ssm_d.reshape(DEPTH, 1, D_SSM)
    cache_k4 = cache_k.reshape(DEPTH, DEC_BATCH, WINDOW, D_KV)
    cache_v4 = cache_v.reshape(DEPTH, DEC_BATCH, WINDOW, D_KV)

    reps = CHUNK // DEC_SEQ
    w_small = jnp.tril(sgu_w[:, :, :DEC_SEQ, :DEC_SEQ])
    eye = jnp.eye(reps, dtype=F32)
    w_sample = (eye[None, None, :, None, :, None] * w_small[:, :, None, :, None, :]).reshape(
        DEPTH, N_SGU_HEADS, CHUNK, CHUNK)
    w_mix = jnp.stack([sgu_w, w_sample], axis=1)
    b_prompt = jnp.repeat(sgu_b.transpose(0, 2, 1), SGU_HEAD_DIM, axis=2)
    b_sample = jnp.tile(b_prompt[:, :DEC_SEQ], (1, reps, 1))
    bias_mix = jnp.stack([b_prompt, b_sample], axis=1)

    a_re, a_im, bbt_re, bbt_im = ssm_discretize(
        ssm_lambda_re, ssm_lambda_im, ssm_log_dt,
        ssm_b_re.transpose(0, 1, 3, 2), ssm_b_im.transpose(0, 1, 3, 2))
    a_re3 = a_re.reshape(DEPTH, 1, N_STATE)
    a_im3 = a_im.reshape(DEPTH, 1, N_STATE)
    hg = N_SSM_GROUPS // 2
    halves = lambda blocks: _block_diag(blocks.reshape(DEPTH, 2, hg, *blocks.shape[2:]))
    bre, bim = halves(bbt_re), halves(bbt_im)
    cre = halves(ssm_c_re.transpose(0, 1, 3, 2))
    cim = halves(ssm_c_im.transpose(0, 1, 3, 2))
    zeros_state = jnp.zeros((BATCH, N_STATE), F32)

    ones_tiles = jnp.ones((T_ALL // ROW_TILE,), jnp.int32)
    ones_scale = jnp.ones((T_ALL, 1), F32)

    outs = {k: [] for k in ("kp", "vp", "hrp", "hip", "ks", "vs", "hrs", "his", "sgs")}
    for l in range(DEPTH):
        z = norm_matmul(x, norm1_g3, w_in, l, tm=ROW_TILE, tn=512)
        sinks = attn_sinks[l]
        attn_n, k_p, v_p = attn_prompt(z, sinks, cos_p, sin_p, mix_g3, l)
        attn_n, k_s, v_s = attn_sample(z, attn_n, cache_k4, cache_v4, sinks, cos_s, sin_s, mix_g3, l)
        sgu_n, sv_s = sgu(z, w_mix, bias_mix, ln_g3, ln_b3, mix_g3, l)
        ssm_args = (bre, bim, cre, cim, a_re3, a_im3, ssm_d3, ssm_glu_w, glu_b3, mix_g3, l)
        ssm_n, hr_p, hi_p = ssm(z, jnp.zeros((T_ALL, D_SSM), BF16), zeros_state, zeros_state, *ssm_args,
                                ns=BATCH, lc=CHUNK, first_block=0, n_blocks=N_CHUNKS)
        ssm_n, hr_s, hi_s = ssm(z, ssm_n, state_ssm_re[l].reshape(DEC_BATCH, N_STATE),
                                state_ssm_im[l].reshape(DEC_BATCH, N_STATE), *ssm_args,
                                ns=DEC_BATCH, lc=DEC_SEQ, first_block=T_PROMPT // T_SAMPLE, n_blocks=1)
        x = out_proj(attn_n, sgu_n, ssm_n, w_out, x, l, tm=ROW_TILE, tn=512)

        outs["kp"].append(k_p.reshape(BATCH, WINDOW, N_KV_HEADS, HEAD_DIM))
        outs["vp"].append(v_p.reshape(BATCH, WINDOW, N_KV_HEADS, HEAD_DIM))
        outs["hrp"].append(hr_p.reshape(BATCH, N_SSM_GROUPS, SSM_STATE))
        outs["hip"].append(hi_p.reshape(BATCH, N_SSM_GROUPS, SSM_STATE))
        outs["ks"].append(k_s.reshape(DEC_BATCH, WINDOW, N_KV_HEADS, HEAD_DIM))
        outs["vs"].append(v_s.reshape(DEC_BATCH, WINDOW, N_KV_HEADS, HEAD_DIM))
        outs["hrs"].append(hr_s.reshape(DEC_BATCH, N_SSM_GROUPS, SSM_STATE))
        outs["his"].append(hi_s.reshape(DEC_BATCH, N_SSM_GROUPS, SSM_STATE))
        outs["sgs"].append(sv_s.reshape(DEC_BATCH, DEC_SEQ, D_SGU))

        i = l // 2
        if l % 2 == 0:
            h2 = norm_rows(x, norm2_g3, l, tm=ROW_TILE, dtype=BF16)
            x = grouped_ffn(h2, jnp.full_like(ones_tiles, i), ones_tiles, ones_scale,
                            ffn_w_gate, ffn_w_up, ffn_w_down, (), x, tm=ROW_TILE, tf=FF_TILE)
        else:
            top_i, top_p = router_top2(x, norm2_g3, moe_router, l, i, tm=ROW_TILE)
            src_token, scale, slots, tile_expert, tile_active = _moe_plan(
                top_i[:, :2], top_p[:, :2], MOE_ROW_TILE, MOE_TILES)
            xs = gather_norm(x, src_token, norm2_g3, l, tg=GATHER_TILE)
            o_sorted = grouped_ffn(xs, tile_expert, tile_active, scale,
                                   moe_w_gate, moe_w_up, moe_w_down, (i,), None,
                                   tm=MOE_ROW_TILE, tf=FF_TILE)
            x = moe_combine(o_sorted, slots, x, tc=GATHER_TILE)

    y_prompt, y_sample = final_norm(x, final_norm_g)
    st = lambda key: jnp.stack(outs[key])
    return (y_prompt.reshape(BATCH, SEQ, d), y_sample.reshape(DEC_BATCH, DEC_SEQ, d),
            st("kp"), st("vp"), st("hrp"), st("hip"), st("ks"), st("vs"), st("hrs"), st("his"), st("sgs"))
```

```python
import functools
import math

import jax
import jax.numpy as jnp
from jax import lax
from jax.experimental import pallas as pl
from jax.experimental.pallas import tpu as pltpu

F32 = jnp.float32
BF16 = jnp.bfloat16

D_MODEL = 2048
BATCH = 4
SEQ = 2048
DEPTH = 4
DEC_BATCH = 32
DEC_SEQ = 8
PAST_LEN = 16384
HEAD_DIM = 64
D_ATTN = 1024
N_Q_HEADS = 16
N_KV_HEADS = 4
Q_PER_KV = 4
D_KV = 256
WINDOW = 128
ROPE_THETA = 10000.0
D_SGU = 512
N_SGU_HEADS = 4
SGU_HEAD_DIM = 128
D_SSM = 512
SSM_GROUP = 16
N_SSM_GROUPS = 32
SSM_STATE = 64
N_STATE = N_SSM_GROUPS * SSM_STATE
D_IN_PROJ = 3072
N_EXPERTS = 8
NORM_EPS = 1e-6
NEG_INF = -1e30
SQRT_HALF = math.sqrt(0.5)

CHUNK = 128
N_CHUNKS = SEQ // CHUNK
T_PROMPT = BATCH * SEQ
T_SAMPLE = DEC_BATCH * DEC_SEQ
T_ALL = T_PROMPT + T_SAMPLE

VMEM_LIMIT_BYTES = 56 * 1024 * 1024

ROW_TILE = 768
DENSE_ROW_BLOCK = ROW_TILE // 2
MOE_ROW_TILE = 1024
MOE_ROW_BLOCK = MOE_ROW_TILE // 2
MOE_TILES = (2 * T_ALL) // MOE_ROW_TILE + N_EXPERTS
FF_TILE = 256
GATHER_TILE = 256


def _params(*sem):
    return pltpu.CompilerParams(dimension_semantics=sem, vmem_limit_bytes=VMEM_LIMIT_BYTES)


def _rms_scale(x):
    return x * lax.rsqrt(jnp.mean(x * x, axis=-1, keepdims=True) + NORM_EPS)


def _gelu(x):
    return 0.5 * x * (1.0 + lax.erf(x * SQRT_HALF))


def _dot(a, b):
    return jnp.dot(a, b, preferred_element_type=F32)


def _norm_matmul_kernel(x_ref, g_ref, w_ref, o_ref, h_ref):
    @pl.when(pl.program_id(1) == 0)
    def _():
        h_ref[...] = (_rms_scale(x_ref[...]) * g_ref[...]).astype(BF16)

    o_ref[...] = _dot(h_ref[...], w_ref[...].astype(BF16))


def norm_matmul(x, g_all, w_all, layer, *, tm, tn):
    t, k = x.shape
    n = w_all.shape[-1]
    return pl.pallas_call(
        _norm_matmul_kernel,
        grid=(t // tm, n // tn),
        in_specs=[
            pl.BlockSpec((tm, k), lambda i, j: (i, 0)),
            pl.BlockSpec((None, 1, k), lambda i, j: (layer, 0, 0)),
            pl.BlockSpec((None, k, tn), lambda i, j: (layer, 0, j)),
        ],
        out_specs=pl.BlockSpec((tm, tn), lambda i, j: (i, j)),
        out_shape=jax.ShapeDtypeStruct((t, n), F32),
        scratch_shapes=[pltpu.VMEM((tm, k), BF16)],
        compiler_params=_params("arbitrary", "arbitrary"),
        name="norm_matmul",
    )(x, g_all, w_all)


def _rope(x, cos, sin_signed):
    lane = lax.broadcasted_iota(jnp.int32, (x.shape[0], 128), 1)
    first_half = (lane % HEAD_DIM) < (HEAD_DIM // 2)
    outs = []
    for j in range(x.shape[1] // 128):
        xc = x[:, j * 128:(j + 1) * 128]
        swapped = jnp.where(first_half, pltpu.roll(xc, 96, 1), pltpu.roll(xc, 32, 1))
        outs.append(xc * cos + swapped * sin_signed)
    return jnp.concatenate(outs, axis=1) if len(outs) > 1 else outs[0]


def _attend(q, kk, vv, mask, sink_ref):
    r = q.shape[0]
    per_dot = Q_PER_KV if r < 64 else 1
    mask_rows = jnp.concatenate([mask] * per_dot, axis=0) if per_dot > 1 else mask
    outs = []
    for kh in range(N_KV_HEADS):
        k = kk[:, kh * HEAD_DIM:(kh + 1) * HEAD_DIM].astype(BF16)
        v = vv[:, kh * HEAD_DIM:(kh + 1) * HEAD_DIM].astype(BF16)
        for first in range(kh * Q_PER_KV, (kh + 1) * Q_PER_KV, per_dot):
            heads = range(first, first + per_dot)
            qs = [q[:, h * HEAD_DIM:(h + 1) * HEAD_DIM] for h in heads]
            sinks = [jnp.full((r, 1), sink_ref[h], F32) for h in heads]
            qh = (jnp.concatenate(qs, axis=0) if per_dot > 1 else qs[0]).astype(BF16)
            sink = jnp.concatenate(sinks, axis=0) if per_dot > 1 else sinks[0]
            s = lax.dot_general(qh, k, (((1,), (1,)), ((), ())), preferred_element_type=F32)
            s = s * (HEAD_DIM ** -0.5)
            s = jnp.where(mask_rows, s, NEG_INF)
            m = jnp.maximum(jnp.max(s, axis=-1, keepdims=True), sink)
            p = jnp.exp(s - m)
            denom = jnp.sum(p, axis=-1, keepdims=True) + jnp.exp(sink - m)
            o = _dot((p / denom).astype(BF16), v)
            outs.extend(o[i * r:(i + 1) * r] for i in range(per_dot))
    return jnp.concatenate(outs, axis=1)


def _attn_prompt_kernel(sink_ref, q_ref, kc_ref, kp_ref, vc_ref, vp_ref, cosc_ref, sinc_ref,
                        cosp_ref, sinp_ref, g_ref, init_ref, o_ref, ks_ref, vs_ref):
    del init_ref
    c = pl.program_id(1)
    q = _rope(q_ref[...], cosc_ref[...], sinc_ref[...])
    kc = _rope(kc_ref[...], cosc_ref[...], sinc_ref[...])
    kp = _rope(kp_ref[...], cosp_ref[...], sinp_ref[...])
    kk = jnp.concatenate([kp, kc], axis=0)
    vv = jnp.concatenate([vp_ref[...], vc_ref[...]], axis=0)
    row = lax.broadcasted_iota(jnp.int32, (CHUNK, 2 * CHUNK), 0)
    col = lax.broadcasted_iota(jnp.int32, (CHUNK, 2 * CHUNK), 1)
    mask = (col > row) & (col <= row + WINDOW) & ((col >= CHUNK) | (c > 0))
    attn = _attend(q, kk, vv, mask, sink_ref)
    o_ref[...] = (_rms_scale(attn) * g_ref[...]).astype(BF16)
    ks_ref[...] = kc
    vs_ref[...] = vc_ref[...]


def attn_prompt(z, sinks, cos_t, sin_t, mix_g_all, layer):
    blk = lambda b, c: c * BATCH + b
    prev = lambda b, c: jnp.maximum(c - 1, 0) * BATCH + b
    kcol, vcol = D_ATTN // D_KV, D_ATTN // D_KV + 1
    return pl.pallas_call(
        _attn_prompt_kernel,
        grid=(BATCH, N_CHUNKS),
        in_specs=[
            pl.BlockSpec(memory_space=pltpu.SMEM),
            pl.BlockSpec((CHUNK, D_ATTN), lambda b, c: (blk(b, c), 0)),
            pl.BlockSpec((CHUNK, D_KV), lambda b, c: (blk(b, c), kcol)),
            pl.BlockSpec((CHUNK, D_KV), lambda b, c: (prev(b, c), kcol)),
            pl.BlockSpec((CHUNK, D_KV), lambda b, c: (blk(b, c), vcol)),
            pl.BlockSpec((CHUNK, D_KV), lambda b, c: (prev(b, c), vcol)),
            pl.BlockSpec((CHUNK, 128), lambda b, c: (c, 0)),
            pl.BlockSpec((CHUNK, 128), lambda b, c: (c, 0)),
            pl.BlockSpec((CHUNK, 128), lambda b, c: (jnp.maximum(c - 1, 0), 0)),
            pl.BlockSpec((CHUNK, 128), lambda b, c: (jnp.maximum(c - 1, 0), 0)),
            pl.BlockSpec((None, 1, D_ATTN), lambda b, c: (layer, 0, 0)),
            pl.BlockSpec(memory_space=pl.ANY),
        ],
        out_specs=[
            pl.BlockSpec((CHUNK, D_ATTN), lambda b, c: (blk(b, c), 0)),
            pl.BlockSpec((None, CHUNK, D_KV), lambda b, c: (b, 0, 0)),
            pl.BlockSpec((None, CHUNK, D_KV), lambda b, c: (b, 0, 0)),
        ],
        out_shape=[
            jax.ShapeDtypeStruct((T_ALL, D_ATTN), BF16),
            jax.ShapeDtypeStruct((BATCH, CHUNK, D_KV), F32),
            jax.ShapeDtypeStruct((BATCH, CHUNK, D_KV), F32),
        ],
        input_output_aliases={11: 0},
        compiler_params=_params("arbitrary", "arbitrary"),
        name="attn_prompt",
    )(sinks, z, z, z, z, z, cos_t, sin_t, cos_t, sin_t, mix_g_all,
      jnp.zeros((T_ALL, D_ATTN), BF16))


SAMPLE_SEQS_PER_STEP = 4


def _attn_sample_kernel(sink_ref, q_ref, kn_ref, vn_ref, ck_ref, cv_ref, cos_ref, sin_ref, g_ref,
                        prev_ref, o_ref, ks_ref, vs_ref):
    del prev_ref
    row = lax.broadcasted_iota(jnp.int32, (DEC_SEQ, WINDOW + DEC_SEQ), 0)
    col = lax.broadcasted_iota(jnp.int32, (DEC_SEQ, WINDOW + DEC_SEQ), 1)
    mask = (col > row) & (col <= row + WINDOW)
    normed = []
    for s in range(SAMPLE_SEQS_PER_STEP):
        rows = slice(s * DEC_SEQ, (s + 1) * DEC_SEQ)
        q = _rope(q_ref[rows, :], cos_ref[...], sin_ref[...])
        kn = _rope(kn_ref[rows, :], cos_ref[...], sin_ref[...])
        vn = vn_ref[rows, :]
        kk = jnp.concatenate([ck_ref[s], kn], axis=0)
        vv = jnp.concatenate([cv_ref[s], vn], axis=0)
        attn = _attend(q, kk, vv, mask, sink_ref)
        normed.append(_rms_scale(attn) * g_ref[...])
        ks_ref[s] = kk[DEC_SEQ:]
        vs_ref[s] = vv[DEC_SEQ:]
    o_ref[...] = jnp.concatenate(normed, axis=0).astype(BF16)


def attn_sample(z, attn_n, cache_k, cache_v, sinks, cos_s, sin_s, mix_g_all, layer):
    nb = SAMPLE_SEQS_PER_STEP
    rows = nb * DEC_SEQ
    base = T_PROMPT // rows
    kcol, vcol = D_ATTN // D_KV, D_ATTN // D_KV + 1
    return pl.pallas_call(
        _attn_sample_kernel,
        grid=(DEC_BATCH // nb,),
        in_specs=[
            pl.BlockSpec(memory_space=pltpu.SMEM),
            pl.BlockSpec((rows, D_ATTN), lambda i: (base + i, 0)),
            pl.BlockSpec((rows, D_KV), lambda i: (base + i, kcol)),
            pl.BlockSpec((rows, D_KV), lambda i: (base + i, vcol)),
            pl.BlockSpec((None, nb, WINDOW, D_KV), lambda i: (layer, i, 0, 0)),
            pl.BlockSpec((None, nb, WINDOW, D_KV), lambda i: (layer, i, 0, 0)),
            pl.BlockSpec((DEC_SEQ, 128), lambda i: (0, 0)),
            pl.BlockSpec((DEC_SEQ, 128), lambda i: (0, 0)),
            pl.BlockSpec((None, 1, D_ATTN), lambda i: (layer, 0, 0)),
            pl.BlockSpec(memory_space=pl.ANY),
        ],
        out_specs=[
            pl.BlockSpec((rows, D_ATTN), lambda i: (base + i, 0)),
            pl.BlockSpec((nb, WINDOW, D_KV), lambda i: (i, 0, 0)),
            pl.BlockSpec((nb, WINDOW, D_KV), lambda i: (i, 0, 0)),
        ],
        out_shape=[
            jax.ShapeDtypeStruct((T_ALL, D_ATTN), BF16),
            jax.ShapeDtypeStruct((DEC_BATCH, WINDOW, D_KV), F32),
            jax.ShapeDtypeStruct((DEC_BATCH, WINDOW, D_KV), F32),
        ],
        input_output_aliases={9: 0},
        compiler_params=_params("arbitrary"),
        name="attn_sample",
    )(sinks, z, z, z, cache_k, cache_v, cos_s, sin_s, mix_g_all, attn_n)


def _sgu_kernel(u_ref, v_ref, w_ref, bias_ref, lng_ref, lnb_ref, g_ref, o_ref, sv_ref):
    u = _gelu(u_ref[...])
    v = _gelu(v_ref[...])
    vc = v - jnp.mean(v, axis=-1, keepdims=True)
    v = vc * lax.rsqrt(jnp.mean(vc * vc, axis=-1, keepdims=True) + NORM_EPS)
    v = v * lng_ref[...] + lnb_ref[...]
    sv_ref[...] = v
    row = lax.broadcasted_iota(jnp.int32, (CHUNK, CHUNK), 0)
    col = lax.broadcasted_iota(jnp.int32, (CHUNK, CHUNK), 1)
    causal = col <= row
    vb = v.astype(BF16)
    mixed = []
    for h in range(N_SGU_HEADS):
        w = jnp.where(causal, w_ref[h], 0.0).astype(BF16)
        mixed.append(_dot(w, vb[:, h * SGU_HEAD_DIM:(h + 1) * SGU_HEAD_DIM]))
    sgu = u * (jnp.concatenate(mixed, axis=1) + bias_ref[...])
    o_ref[...] = (_rms_scale(sgu) * g_ref[...]).astype(BF16)


def sgu(z, w_mix, bias_mix, ln_g_all, ln_b_all, mix_g_all, layer):
    n_prompt = T_PROMPT // CHUNK
    kind = lambda i: jnp.where(i >= n_prompt, 1, 0)
    ucol, vcol = (D_ATTN + 2 * D_KV) // D_SGU, (D_ATTN + 2 * D_KV) // D_SGU + 1
    gcol = D_ATTN // D_SGU
    return pl.pallas_call(
        _sgu_kernel,
        grid=(T_ALL // CHUNK,),
        in_specs=[
            pl.BlockSpec((CHUNK, D_SGU), lambda i: (i, ucol)),
            pl.BlockSpec((CHUNK, D_SGU), lambda i: (i, vcol)),
            pl.BlockSpec((None, None, N_SGU_HEADS, CHUNK, CHUNK), lambda i: (layer, kind(i), 0, 0, 0)),
            pl.BlockSpec((None, None, CHUNK, D_SGU), lambda i: (layer, kind(i), 0, 0)),
            pl.BlockSpec((None, 1, D_SGU), lambda i: (layer, 0, 0)),
            pl.BlockSpec((None, 1, D_SGU), lambda i: (layer, 0, 0)),
            pl.BlockSpec((None, 1, D_SGU), lambda i: (layer, 0, gcol)),
        ],
        out_specs=[
            pl.BlockSpec((CHUNK, D_SGU), lambda i: (i, 0)),
            pl.BlockSpec((CHUNK, D_SGU), lambda i: (jnp.maximum(i - n_prompt, 0), 0)),
        ],
        out_shape=[
            jax.ShapeDtypeStruct((T_ALL, D_SGU), BF16),
            jax.ShapeDtypeStruct((T_SAMPLE, D_SGU), F32),
        ],
        compiler_params=_params("arbitrary"),
        name="sgu",
    )(z, z, w_mix, bias_mix, ln_g_all, ln_b_all, mix_g_all)


def _ssm_disc_kernel(lr_ref, li_ref, ldt_ref, br_ref, bi_ref, ar_ref, ai_ref, bbr_ref, bbi_ref):
    lr = lr_ref[...]
    li = li_ref[...]
    dt = jnp.exp(ldt_ref[...])
    mag = jnp.exp(lr * dt)
    a_re = mag * jnp.cos(li * dt)
    a_im = mag * jnp.sin(li * dt)
    nr, ni = a_re - 1.0, a_im
    den = lr * lr + li * li
    c_re = (nr * lr + ni * li) / den
    c_im = (ni * lr - nr * li) / den
    ar_ref[...] = a_re
    ai_ref[...] = a_im
    bbr_ref[...] = c_re * br_ref[...] - c_im * bi_ref[...]
    bbi_ref[...] = c_re * bi_ref[...] + c_im * br_ref[...]


def ssm_discretize(lam_re, lam_im, log_dt, b_re_t, b_im_t):
    d, g, p = lam_re.shape
    c = b_re_t.shape[2]
    return pl.pallas_call(
        _ssm_disc_kernel,
        out_shape=[
            jax.ShapeDtypeStruct((d, g, 1, p), F32),
            jax.ShapeDtypeStruct((d, g, 1, p), F32),
            jax.ShapeDtypeStruct((d, g, c, p), F32),
            jax.ShapeDtypeStruct((d, g, c, p), F32),
        ],
        compiler_params=pltpu.CompilerParams(vmem_limit_bytes=VMEM_LIMIT_BYTES),
        name="ssm_discretize",
    )(lam_re.reshape(d, g, 1, p), lam_im.reshape(d, g, 1, p), log_dt.reshape(d, g, 1, 1),
      b_re_t, b_im_t)


SCAN_COLS = 512
SCAN_BLOCKS = SCAN_COLS // 128


def _ssm_kernel(u_ref, bre_ref, bim_ref, cre_ref, cim_ref, are_ref, aim_ref, d_ref, gw_ref, gb_ref,
                g_ref, h0r_ref, h0i_ref, prev_ref, o_ref, hr_ref, hi_ref, xr_ref, xi_ref, *, ns, lc):
    del prev_ref
    half_u = D_SSM // 2
    half_s = N_STATE // 2

    @pl.when(pl.program_id(0) == 0)
    def _():
        hr_ref[...] = h0r_ref[...]
        hi_ref[...] = h0i_ref[...]

    u = u_ref[...]
    ub = u.astype(BF16)
    blocks_per_half = half_s // 128
    for hf in range(2):
        uh = ub[:, hf * half_u:(hf + 1) * half_u]
        x_re = _dot(uh, bre_ref[hf].astype(BF16))
        x_im = _dot(uh, bim_ref[hf].astype(BF16))
        for j in range(blocks_per_half):
            xr_ref[hf * blocks_per_half + j] = x_re[:, j * 128:(j + 1) * 128]
            xi_ref[hf * blocks_per_half + j] = x_im[:, j * 128:(j + 1) * 128]

    for cc in range(N_STATE // SCAN_COLS):
        blocks = range(cc * SCAN_BLOCKS, (cc + 1) * SCAN_BLOCKS)
        a_re = [jnp.broadcast_to(are_ref[:, j * 128:(j + 1) * 128], (ns, 128)) for j in blocks]
        a_im = [jnp.broadcast_to(aim_ref[:, j * 128:(j + 1) * 128], (ns, 128)) for j in blocks]

        def step(t, carry):
            rows = pl.ds(t, ns, stride=lc)
            new = []
            for n, j in enumerate(blocks):
                h_re, h_im = carry[n]
                n_re = a_re[n] * h_re - a_im[n] * h_im + xr_ref[j, rows, :]
                n_im = a_re[n] * h_im + a_im[n] * h_re + xi_ref[j, rows, :]
                xr_ref[j, rows, :] = n_re
                xi_ref[j, rows, :] = n_im
                new.append((n_re, n_im))
            return tuple(new)

        init = tuple((hr_ref[:, j * 128:(j + 1) * 128], hi_ref[:, j * 128:(j + 1) * 128]) for j in blocks)
        final = lax.fori_loop(0, lc, step, init)
        for n, j in enumerate(blocks):
            hr_ref[:, j * 128:(j + 1) * 128] = final[n][0]
            hi_ref[:, j * 128:(j + 1) * 128] = final[n][1]

    ys = []
    for hf in range(2):
        js = range(hf * blocks_per_half, (hf + 1) * blocks_per_half)
        h_re = jnp.concatenate([xr_ref[j].astype(BF16) for j in js], axis=1)
        h_im = jnp.concatenate([xi_ref[j].astype(BF16) for j in js], axis=1)
        ys.append(_dot(h_re, cre_ref[hf].astype(BF16)) - _dot(h_im, cim_ref[hf].astype(BF16)))
    y = jnp.concatenate(ys, axis=1) + d_ref[...] * u
    g = _gelu(y)
    gate = _dot(g.astype(BF16), gw_ref[...].astype(BF16)) + gb_ref[...]
    ssm = g * jax.nn.sigmoid(gate)
    o_ref[...] = (_rms_scale(ssm) * g_ref[...]).astype(BF16)


def ssm(z, prev, h0_re, h0_im, bre, bim, cre, cim, a_re, a_im, d_all, glu_w_all, glu_b_all,
        mix_g_all, layer, *, ns, lc, first_block, n_blocks):
    rows = ns * lc
    zcol = (D_ATTN + 2 * D_KV + 2 * D_SGU) // D_SSM
    gcol = (D_ATTN + D_SGU) // D_SSM
    full = lambda shape: pl.BlockSpec(shape, lambda i: (0,) * len(shape))
    lay3 = lambda shape: pl.BlockSpec((None,) + shape, lambda i: (layer,) + (0,) * len(shape))
    in_specs = [
        pl.BlockSpec((rows, D_SSM), lambda i: (first_block + i, zcol)),
        lay3((2, D_SSM // 2, N_STATE // 2)),
        lay3((2, D_SSM // 2, N_STATE // 2)),
        lay3((2, N_STATE // 2, D_SSM // 2)),
        lay3((2, N_STATE // 2, D_SSM // 2)),
        lay3((1, N_STATE)),
        lay3((1, N_STATE)),
        lay3((1, D_SSM)),
        lay3((D_SSM, D_SSM)),
        lay3((1, D_SSM)),
        pl.BlockSpec((None, 1, D_SSM), lambda i: (layer, 0, gcol)),
        full((ns, N_STATE)),
        full((ns, N_STATE)),
        pl.BlockSpec(memory_space=pl.ANY),
    ]
    args = [z, bre, bim, cre, cim, a_re, a_im, d_all, glu_w_all, glu_b_all, mix_g_all, h0_re, h0_im,
            prev]
    return pl.pallas_call(
        functools.partial(_ssm_kernel, ns=ns, lc=lc),
        grid=(n_blocks,),
        in_specs=in_specs,
        out_specs=[
            pl.BlockSpec((rows, D_SSM), lambda i: (first_block + i, 0)),
            full((ns, N_STATE)),
            full((ns, N_STATE)),
        ],
        out_shape=[
            jax.ShapeDtypeStruct((T_ALL, D_SSM), BF16),
            jax.ShapeDtypeStruct((ns, N_STATE), F32),
            jax.ShapeDtypeStruct((ns, N_STATE), F32),
        ],
        scratch_shapes=[pltpu.VMEM((N_STATE // 128, rows, 128), F32),
                        pltpu.VMEM((N_STATE // 128, rows, 128), F32)],
        input_output_aliases={13: 0},
        compiler_params=_params("arbitrary"),
        name="ssm",
    )(*args)


def _out_proj_kernel(a0_ref, a1_ref, a2_ref, w_ref, r_ref, o_ref):
    k0 = a0_ref.shape[1]
    k1 = k0 + a1_ref.shape[1]
    acc = _dot(a0_ref[...], w_ref[:k0, :].astype(BF16))
    acc += _dot(a1_ref[...], w_ref[k0:k1, :].astype(BF16))
    acc += _dot(a2_ref[...], w_ref[k1:, :].astype(BF16))
    o_ref[...] = r_ref[...] + acc


def out_proj(a0, a1, a2, w_all, resid, layer, *, tm, tn):
    t, n = resid.shape
    k = w_all.shape[1]
    return pl.pallas_call(
        _out_proj_kernel,
        grid=(t // tm, n // tn),
        in_specs=[
            pl.BlockSpec((tm, a0.shape[1]), lambda i, j: (i, 0)),
            pl.BlockSpec((tm, a1.shape[1]), lambda i, j: (i, 0)),
            pl.BlockSpec((tm, a2.shape[1]), lambda i, j: (i, 0)),
            pl.BlockSpec((None, k, tn), lambda i, j: (layer, 0, j)),
            pl.BlockSpec((tm, tn), lambda i, j: (i, j)),
        ],
        out_specs=pl.BlockSpec((tm, tn), lambda i, j: (i, j)),
        out_shape=jax.ShapeDtypeStruct((t, n), F32),
        compiler_params=_params("arbitrary", "arbitrary"),
        name="out_proj",
    )(a0, a1, a2, w_all, resid)


def _norm_kernel(x_ref, g_ref, o_ref):
    o_ref[...] = (_rms_scale(x_ref[...]) * g_ref[...]).astype(o_ref.dtype)


def norm_rows(x, g_all, layer, *, tm, dtype):
    t, k = x.shape
    return pl.pallas_call(
        _norm_kernel,
        grid=(t // tm,),
        in_specs=[pl.BlockSpec((tm, k), lambda i: (i, 0)),
                  pl.BlockSpec((None, 1, k), lambda i: (layer, 0, 0))],
        out_specs=pl.BlockSpec((tm, k), lambda i: (i, 0)),
        out_shape=jax.ShapeDtypeStruct((t, k), dtype),
        compiler_params=_params("arbitrary"),
        name="norm_rows",
    )(x, g_all)


def _router_kernel(x_ref, g_ref, w_ref, idx_ref, p_ref):
    h = (_rms_scale(x_ref[...]) * g_ref[...]).astype(BF16)
    logits = _dot(h, w_ref[...].astype(BF16))
    e = jnp.exp(logits - jnp.max(logits, axis=-1, keepdims=True))
    probs = e / jnp.sum(e, axis=-1, keepdims=True)
    lane = lax.broadcasted_iota(jnp.int32, probs.shape, 1)
    p1 = jnp.max(probs, axis=-1, keepdims=True)
    i1 = jnp.min(jnp.where(probs == p1, lane, N_EXPERTS), axis=-1, keepdims=True)
    rest = jnp.where(lane == i1, -1.0, probs)
    p2 = jnp.max(rest, axis=-1, keepdims=True)
    i2 = jnp.min(jnp.where(rest == p2, lane, N_EXPERTS), axis=-1, keepdims=True)
    tot = p1 + p2
    idx_ref[...] = jnp.where(lane == 0, i1, jnp.where(lane == 1, i2, 0))
    p_ref[...] = jnp.where(lane == 0, p1 / tot, jnp.where(lane == 1, p2 / tot, 0.0))


def router_top2(x, g_all, w_all, layer, moe_layer, *, tm):
    t, k = x.shape
    return pl.pallas_call(
        _router_kernel,
        grid=(t // tm,),
        in_specs=[pl.BlockSpec((tm, k), lambda i: (i, 0)),
                  pl.BlockSpec((None, 1, k), lambda i: (layer, 0, 0)),
                  pl.BlockSpec((None, k, N_EXPERTS), lambda i: (moe_layer, 0, 0))],
        out_specs=[pl.BlockSpec((tm, N_EXPERTS), lambda i: (i, 0)),
                   pl.BlockSpec((tm, N_EXPERTS), lambda i: (i, 0))],
        out_shape=[jax.ShapeDtypeStruct((t, N_EXPERTS), jnp.int32),
                   jax.ShapeDtypeStruct((t, N_EXPERTS), F32)],
        compiler_params=_params("arbitrary"),
        name="router_top2",
    )(x, g_all, w_all)


def _row_copy(src_hbm, row, dst, r, sem):
    return pltpu.make_async_copy(src_hbm.at[pl.ds(row, 1)], dst.at[pl.ds(r, 1)], sem)


def _gather_norm_kernel(idx_ref, act_ref, x_hbm, g_ref, o_ref, buf, sems):
    tg = buf.shape[1]
    i = pl.program_id(0)
    n = pl.num_programs(0)

    def copies(tile, fn):
        slot = tile % 2
        base = tile * tg

        def body(r, _):
            fn(_row_copy(x_hbm, idx_ref[base + r], buf.at[slot], r, sems.at[slot]))
            return 0

        lax.fori_loop(0, tg, body, 0)

    @pl.when(jnp.logical_and(i == 0, act_ref[0] > 0))
    def _():
        copies(0, lambda cp: cp.start())

    @pl.when(jnp.logical_and(i + 1 < n, act_ref[jnp.minimum(i + 1, n - 1)] > 0))
    def _():
        copies(i + 1, lambda cp: cp.start())

    @pl.when(act_ref[i] > 0)
    def _():
        copies(i, lambda cp: cp.wait())
        o_ref[...] = (_rms_scale(buf[i % 2]) * g_ref[...]).astype(BF16)

    @pl.when(act_ref[i] == 0)
    def _():
        o_ref[...] = jnp.zeros_like(o_ref)


def gather_norm(x, idx, tile_active, g_all, layer, *, tg):
    n = idx.shape[0]
    k = x.shape[1]
    return pl.pallas_call(
        _gather_norm_kernel,
        grid_spec=pltpu.PrefetchScalarGridSpec(
            num_scalar_prefetch=2,
            grid=(n // tg,),
            in_specs=[pl.BlockSpec(memory_space=pl.ANY),
                      pl.BlockSpec((None, 1, k), lambda i, idx, act: (layer, 0, 0))],
            out_specs=pl.BlockSpec((tg, k), lambda i, idx, act: (i, 0)),
            scratch_shapes=[pltpu.VMEM((2, tg, k), F32), pltpu.SemaphoreType.DMA((2,))],
        ),
        out_shape=jax.ShapeDtypeStruct((n, k), BF16),
        compiler_params=_params("arbitrary"),
        name="gather_norm",
    )(idx, tile_active, x, g_all)


def _combine_kernel(slot_ref, o_hbm, x_ref, out_ref, buf_a, buf_b, sems):
    tc = buf_a.shape[1]
    i = pl.program_id(0)
    n = pl.num_programs(0)

    def copies(tile, fn):
        slot = tile % 2
        base = tile * tc

        def body(r, _):
            fn(_row_copy(o_hbm, slot_ref[2 * (base + r)], buf_a.at[slot], r, sems.at[slot]))
            fn(_row_copy(o_hbm, slot_ref[2 * (base + r) + 1], buf_b.at[slot], r, sems.at[slot]))
            return 0

        lax.fori_loop(0, tc, body, 0)

    @pl.when(i == 0)
    def _():
        copies(0, lambda cp: cp.start())

    @pl.when(i + 1 < n)
    def _():
        copies(i + 1, lambda cp: cp.start())

    copies(i, lambda cp: cp.wait())
    out_ref[...] = x_ref[...] + (buf_a[i % 2] + buf_b[i % 2])


def moe_combine(o_sorted, slots, x, *, tc):
    t, k = x.shape
    return pl.pallas_call(
        _combine_kernel,
        grid_spec=pltpu.PrefetchScalarGridSpec(
            num_scalar_prefetch=1,
            grid=(t // tc,),
            in_specs=[pl.BlockSpec(memory_space=pl.ANY),
                      pl.BlockSpec((tc, k), lambda i, s: (i, 0))],
            out_specs=pl.BlockSpec((tc, k), lambda i, s: (i, 0)),
            scratch_shapes=[pltpu.VMEM((2, tc, k), F32), pltpu.VMEM((2, tc, k), F32),
                            pltpu.SemaphoreType.DMA((2,))],
        ),
        out_shape=jax.ShapeDtypeStruct((t, k), F32),
        compiler_params=_params("arbitrary"),
        name="moe_combine",
    )(slots, o_sorted, x)


def _ffn_kernel(te_ref, nsub_ref, x_ref, wg_ref, wu_ref, wd_ref, s_ref, *rest, nf, rb, has_resid):
    if has_resid:
        r_ref, o_ref = rest
    else:
        (o_ref,) = rest
    f = pl.program_id(1)

    @pl.when(f == 0)
    def _():
        o_ref[...] = jnp.zeros_like(o_ref)

    def row_block(r, carry):
        rows = pl.ds(pl.multiple_of(r * rb, rb), rb)
        x = x_ref[rows, :]
        gate = _dot(x, wg_ref[...].astype(BF16))
        up = _dot(x, wu_ref[...].astype(BF16))
        h = (gate * jax.nn.sigmoid(gate) * up).astype(BF16)
        o_ref[rows, :] += _dot(h, wd_ref[...].astype(BF16))
        return carry

    lax.fori_loop(0, nsub_ref[pl.program_id(0)], row_block, 0)

    @pl.when(f == nf - 1)
    def _():
        out = o_ref[...] * s_ref[...]
        if has_resid:
            out = r_ref[...] + out
        o_ref[...] = out


def grouped_ffn(x_rows, tile_expert, tile_nsub, scale, wg_all, wu_all, wd_all, lead, resid,
                *, tm, tf, rb):
    s, k = x_rows.shape
    ff = wg_all.shape[-1]
    nf = ff // tf
    n_tiles = s // tm
    nlead = len(lead)
    fblk = lambda i, f, act: jnp.where(act[i] > 0, f, nf - 1)
    w_in_spec = pl.BlockSpec((None,) * (nlead + 1) + (k, tf),
                             lambda i, f, te, act: lead + (te[i], 0, fblk(i, f, act)))
    w_out_spec = pl.BlockSpec((None,) * (nlead + 1) + (tf, k),
                              lambda i, f, te, act: lead + (te[i], fblk(i, f, act), 0))
    row_spec = lambda width: pl.BlockSpec((tm, width), lambda i, f, te, act: (i, 0))
    in_specs = [row_spec(k), w_in_spec, w_in_spec, w_out_spec, row_spec(1)]
    args = [x_rows, wg_all, wu_all, wd_all, scale]
    if resid is not None:
        in_specs.append(row_spec(k))
        args.append(resid)
    return pl.pallas_call(
        functools.partial(_ffn_kernel, nf=nf, rb=rb, has_resid=resid is not None),
        grid_spec=pltpu.PrefetchScalarGridSpec(
            num_scalar_prefetch=2,
            grid=(n_tiles, nf),
            in_specs=in_specs,
            out_specs=row_spec(k),
        ),
        out_shape=jax.ShapeDtypeStruct((s, k), F32),
        compiler_params=_params("arbitrary", "arbitrary"),
        name="grouped_ffn",
    )(tile_expert, tile_nsub, *args)


def _moe_plan(top_i, top_p, tm, rb, tg, n_tiles):
    t = top_i.shape[0]
    e = top_i.reshape(-1)
    p = top_p.reshape(-1)
    onehot = (e[:, None] == jnp.arange(N_EXPERTS, dtype=jnp.int32)[None, :]).astype(jnp.int32)
    csum = jnp.cumsum(onehot, axis=0)
    rank = jnp.take_along_axis(csum, e[:, None], axis=1)[:, 0] - 1
    counts = csum[-1]
    tiles_e = (counts + tm - 1) // tm
    tile_end = jnp.cumsum(tiles_e)
    tile_start = tile_end - tiles_e
    dest = tile_start[e] * tm + rank
    src_entry = jnp.full((n_tiles * tm,), -1, jnp.int32).at[dest].set(
        jnp.arange(2 * t, dtype=jnp.int32), unique_indices=True)
    entry = jnp.maximum(src_entry, 0)
    src_token = entry // 2
    scale = jnp.where(src_entry >= 0, p[entry], 0.0)
    tile_ids = jnp.arange(n_tiles, dtype=jnp.int32)
    tile_expert = jnp.sum((tile_ids[:, None] >= tile_end[None, :]).astype(jnp.int32), axis=1)
    active = (tile_ids < tile_end[-1]).astype(jnp.int32)
    last_expert = jnp.minimum(tile_expert[jnp.maximum(tile_end[-1] - 1, 0)], N_EXPERTS - 1)
    tile_expert = jnp.where(active > 0, jnp.minimum(tile_expert, N_EXPERTS - 1), last_expert)
    tile_valid = jnp.clip(counts[tile_expert] - (tile_ids - tile_start[tile_expert]) * tm, 0, tm) * active
    tile_nsub = (tile_valid + rb - 1) // rb
    per = tm // tg
    g_ids = jnp.arange(n_tiles * per, dtype=jnp.int32)
    g_active = (tile_valid[g_ids // per] > (g_ids % per) * tg).astype(jnp.int32)
    return (src_token, scale.reshape(-1, 1), dest.astype(jnp.int32), tile_expert.astype(jnp.int32),
            tile_nsub.astype(jnp.int32), g_active)


def _final_norm_kernel(x_ref, g_ref, op_ref, os_ref):
    i = pl.program_id(0)
    y = _rms_scale(x_ref[...]) * g_ref[...]

    @pl.when(i < T_PROMPT // CHUNK)
    def _():
        op_ref[...] = y

    @pl.when(i >= T_PROMPT // CHUNK)
    def _():
        os_ref[...] = y


def final_norm(x, g):
    n_prompt = T_PROMPT // CHUNK
    k = x.shape[1]

    def prompt_block(i):
        j = jnp.minimum(i, n_prompt - 1)
        return ((j % BATCH) * N_CHUNKS + j // BATCH, 0)

    return pl.pallas_call(
        _final_norm_kernel,
        grid=(T_ALL // CHUNK,),
        in_specs=[pl.BlockSpec((CHUNK, k), lambda i: (i, 0)),
                  pl.BlockSpec((1, k), lambda i: (0, 0))],
        out_specs=[pl.BlockSpec((CHUNK, k), prompt_block),
                   pl.BlockSpec((CHUNK, k), lambda i: (jnp.maximum(i - n_prompt, 0), 0))],
        out_shape=[jax.ShapeDtypeStruct((T_PROMPT, k), F32),
                   jax.ShapeDtypeStruct((T_SAMPLE, k), F32)],
        compiler_params=_params("arbitrary"),
        name="final_norm",
    )(x, g.reshape(1, k))


def _rope_tables(pos):
    half = HEAD_DIM // 2
    inv_freq = jnp.power(ROPE_THETA, -jnp.arange(half, dtype=F32) / half)
    ang = pos.astype(F32)[:, None] * inv_freq[None, :]
    cos, sin = jnp.cos(ang), jnp.sin(ang)
    return jnp.tile(cos, (1, 4)), jnp.tile(jnp.concatenate([-sin, sin], axis=1), (1, 2))


def _block_diag(blocks):
    *lead, n, r, c = blocks.shape
    eye = jnp.eye(n, dtype=blocks.dtype)
    return (blocks[..., :, :, None, :] * eye[:, None, :, None]).reshape(*lead, n * r, n * c)


def kernel(x_prompt, x_sample, cache_k, cache_v, state_ssm_re, state_ssm_im, norm1_g, w_in, attn_sinks, sgu_ln_g, sgu_ln_b, sgu_w, sgu_b, ssm_lambda_re, ssm_lambda_im, ssm_log_dt, ssm_b_re, ssm_b_im, ssm_c_re, ssm_c_im, ssm_d, ssm_glu_w, ssm_glu_b, mix_norm_g, w_out, norm2_g, ffn_w_gate, ffn_w_up, ffn_w_down, moe_router, moe_w_gate, moe_w_up, moe_w_down, final_norm_g):
    d = D_MODEL
    xp = x_prompt.reshape(BATCH, N_CHUNKS, CHUNK, d).transpose(1, 0, 2, 3).reshape(T_PROMPT, d)
    x = jnp.concatenate([xp, x_sample.reshape(T_SAMPLE, d)], axis=0)

    cos_p, sin_p = _rope_tables(jnp.arange(SEQ, dtype=jnp.int32))
    cos_s, sin_s = _rope_tables(PAST_LEN + jnp.arange(DEC_SEQ, dtype=jnp.int32))

    row3 = lambda a: a.reshape(DEPTH, 1, a.shape[-1])
    norm1_g3, norm2_g3, mix_g3 = row3(norm1_g), row3(norm2_g), row3(mix_norm_g)
    ln_g3, ln_b3 = row3(sgu_ln_g), row3(sgu_ln_b)
    glu_b3 = row3(ssm_glu_b)
    ssm_d3 = ssm_d.reshape(DEPTH, 1, D_SSM)
    cache_k4 = cache_k.reshape(DEPTH, DEC_BATCH, WINDOW, D_KV)
    cache_v4 = cache_v.reshape(DEPTH, DEC_BATCH, WINDOW, D_KV)

    reps = CHUNK // DEC_SEQ
    w_small = jnp.tril(sgu_w[:, :, :DEC_SEQ, :DEC_SEQ])
    eye = jnp.eye(reps, dtype=F32)
    w_sample = (eye[None, None, :, None, :, None] * w_small[:, :, None, :, None, :]).reshape(
        DEPTH, N_SGU_HEADS, CHUNK, CHUNK)
    w_mix = jnp.stack([sgu_w, w_sample], axis=1)
    b_prompt = jnp.repeat(sgu_b.transpose(0, 2, 1), SGU_HEAD_DIM, axis=2)
    b_sample = jnp.tile(b_prompt[:, :DEC_SEQ], (1, reps, 1))
    bias_mix = jnp.stack([b_prompt, b_sample], axis=1)

    a_re, a_im, bbt_re, bbt_im = ssm_discretize(
        ssm_lambda_re, ssm_lambda_im, ssm_log_dt,
        ssm_b_re.transpose(0, 1, 3, 2), ssm_b_im.transpose(0, 1, 3, 2))
    a_re3 = a_re.reshape(DEPTH, 1, N_STATE)
    a_im3 = a_im.reshape(DEPTH, 1, N_STATE)
    hg = N_SSM_GROUPS // 2
    halves = lambda blocks: _block_diag(blocks.reshape(DEPTH, 2, hg, *blocks.shape[2:]))
    bre, bim = halves(bbt_re), halves(bbt_im)
    cre = halves(ssm_c_re.transpose(0, 1, 3, 2))
    cim = halves(ssm_c_im.transpose(0, 1, 3, 2))
    zeros_state = jnp.zeros((BATCH, N_STATE), F32)

    ones_tiles = jnp.ones((T_ALL // ROW_TILE,), jnp.int32)
    ones_scale = jnp.ones((T_ALL, 1), F32)

    outs = {k: [] for k in ("kp", "vp", "hrp", "hip", "ks", "vs", "hrs", "his", "sgs")}
    for l in range(DEPTH):
        z = norm_matmul(x, norm1_g3, w_in, l, tm=ROW_TILE, tn=512)
        sinks = attn_sinks[l]
        attn_n, k_p, v_p = attn_prompt(z, sinks, cos_p, sin_p, mix_g3, l)
        attn_n, k_s, v_s = attn_sample(z, attn_n, cache_k4, cache_v4, sinks, cos_s, sin_s, mix_g3, l)
        sgu_n, sv_s = sgu(z, w_mix, bias_mix, ln_g3, ln_b3, mix_g3, l)
        ssm_args = (bre, bim, cre, cim, a_re3, a_im3, ssm_d3, ssm_glu_w, glu_b3, mix_g3, l)
        ssm_n, hr_p, hi_p = ssm(z, jnp.zeros((T_ALL, D_SSM), BF16), zeros_state, zeros_state, *ssm_args,
                                ns=BATCH, lc=CHUNK, first_block=0, n_blocks=N_CHUNKS)
        ssm_n, hr_s, hi_s = ssm(z, ssm_n, state_ssm_re[l].reshape(DEC_BATCH, N_STATE),
                                state_ssm_im[l].reshape(DEC_BATCH, N_STATE), *ssm_args,
                                ns=DEC_BATCH, lc=DEC_SEQ, first_block=T_PROMPT // T_SAMPLE, n_blocks=1)
        x = out_proj(attn_n, sgu_n, ssm_n, w_out, x, l, tm=ROW_TILE, tn=512)

        outs["kp"].append(k_p.reshape(BATCH, WINDOW, N_KV_HEADS, HEAD_DIM))
        outs["vp"].append(v_p.reshape(BATCH, WINDOW, N_KV_HEADS, HEAD_DIM))
        outs["hrp"].append(hr_p.reshape(BATCH, N_SSM_GROUPS, SSM_STATE))
        outs["hip"].append(hi_p.reshape(BATCH, N_SSM_GROUPS, SSM_STATE))
        outs["ks"].append(k_s.reshape(DEC_BATCH, WINDOW, N_KV_HEADS, HEAD_DIM))
        outs["vs"].append(v_s.reshape(DEC_BATCH, WINDOW, N_KV_HEADS, HEAD_DIM))
        outs["hrs"].append(hr_s.reshape(DEC_BATCH, N_SSM_GROUPS, SSM_STATE))
        outs["his"].append(hi_s.reshape(DEC_BATCH, N_SSM_GROUPS, SSM_STATE))
        outs["sgs"].append(sv_s.reshape(DEC_BATCH, DEC_SEQ, D_SGU))

        i = l // 2
        if l % 2 == 0:
            h2 = norm_rows(x, norm2_g3, l, tm=ROW_TILE, dtype=BF16)
            x = grouped_ffn(h2, jnp.full_like(ones_tiles, i), ones_tiles * (ROW_TILE // DENSE_ROW_BLOCK),
                            ones_scale, ffn_w_gate, ffn_w_up, ffn_w_down, (), x,
                            tm=ROW_TILE, tf=FF_TILE, rb=DENSE_ROW_BLOCK)
        else:
            top_i, top_p = router_top2(x, norm2_g3, moe_router, l, i, tm=ROW_TILE)
            src_token, scale, slots, tile_expert, tile_nsub, gather_active = _moe_plan(
                top_i[:, :2], top_p[:, :2], MOE_ROW_TILE, MOE_ROW_BLOCK, GATHER_TILE, MOE_TILES)
            xs = gather_norm(x, src_token, gather_active, norm2_g3, l, tg=GATHER_TILE)
            o_sorted = grouped_ffn(xs, tile_expert, tile_nsub, scale,
                                   moe_w_gate, moe_w_up, moe_w_down, (i,), None,
                                   tm=MOE_ROW_TILE, tf=FF_TILE, rb=MOE_ROW_BLOCK)
            x = moe_combine(o_sorted, slots, x, tc=GATHER_TILE)

    y_prompt, y_sample = final_norm(x, final_norm_g)
    st = lambda key: jnp.stack(outs[key])
    return (y_prompt.reshape(BATCH, SEQ, d), y_sample.reshape(DEC_BATCH, DEC_SEQ, d),
            st("kp"), st("vp"), st("hrp"), st("hip"), st("ks"), st("vs"), st("hrs"), st("his"), st("sgs"))
```

```python
import functools
import math

import jax
import jax.numpy as jnp
from jax import lax
from jax.experimental import pallas as pl
from jax.experimental.pallas import tpu as pltpu

F32 = jnp.float32
BF16 = jnp.bfloat16

D_MODEL = 2048
BATCH = 4
SEQ = 2048
DEPTH = 4
DEC_BATCH = 32
DEC_SEQ = 8
PAST_LEN = 16384
HEAD_DIM = 64
D_ATTN = 1024
N_Q_HEADS = 16
N_KV_HEADS = 4
Q_PER_KV = 4
D_KV = 256
WINDOW = 128
ROPE_THETA = 10000.0
D_SGU = 512
N_SGU_HEADS = 4
SGU_HEAD_DIM = 128
D_SSM = 512
SSM_GROUP = 16
N_SSM_GROUPS = 32
SSM_STATE = 64
N_STATE = N_SSM_GROUPS * SSM_STATE
D_IN_PROJ = 3072
N_EXPERTS = 8
NORM_EPS = 1e-6
NEG_INF = -1e30
SQRT_HALF = math.sqrt(0.5)

CHUNK = 128
N_CHUNKS = SEQ // CHUNK
T_PROMPT = BATCH * SEQ
T_SAMPLE = DEC_BATCH * DEC_SEQ
T_ALL = T_PROMPT + T_SAMPLE

VMEM_LIMIT_BYTES = 56 * 1024 * 1024

ROW_TILE = 768
DENSE_ROW_BLOCK = ROW_TILE // 2
MOE_ROW_TILE = 1024
MOE_ROW_BLOCK = MOE_ROW_TILE // 2
MOE_TILES = (2 * T_ALL) // MOE_ROW_TILE + N_EXPERTS
FF_TILE = 256
GATHER_TILE = 256
DMA_ISSUE_UNROLL = 8


def _params(*sem):
    return pltpu.CompilerParams(dimension_semantics=sem, vmem_limit_bytes=VMEM_LIMIT_BYTES)


def _rms_scale(x):
    return x * lax.rsqrt(jnp.mean(x * x, axis=-1, keepdims=True) + NORM_EPS)


def _gelu(x):
    return 0.5 * x * (1.0 + lax.erf(x * SQRT_HALF))


def _dot(a, b):
    return jnp.dot(a, b, preferred_element_type=F32)


def _norm_matmul_kernel(x_ref, g_ref, w_ref, o_ref, *, tn):
    h = (_rms_scale(x_ref[...]) * g_ref[...]).astype(BF16)
    for j in range(o_ref.shape[1] // tn):
        o_ref[:, j * tn:(j + 1) * tn] = _dot(h, w_ref[:, j * tn:(j + 1) * tn])


def _resident(block_shape, index_map):
    return pl.BlockSpec(block_shape, index_map, pipeline_mode=pl.Buffered(1))


def norm_matmul(x, g_all, w_all, layer, *, tm, tn):
    t, k = x.shape
    n = w_all.shape[-1]
    return pl.pallas_call(
        functools.partial(_norm_matmul_kernel, tn=tn),
        grid=(t // tm,),
        in_specs=[
            pl.BlockSpec((tm, k), lambda i: (i, 0)),
            pl.BlockSpec((None, 1, k), lambda i: (layer, 0, 0)),
            _resident((None, k, n), lambda i: (layer, 0, 0)),
        ],
        out_specs=pl.BlockSpec((tm, n), lambda i: (i, 0)),
        out_shape=jax.ShapeDtypeStruct((t, n), F32),
        compiler_params=_params("arbitrary"),
        name="norm_matmul",
    )(x, g_all, w_all)


def _rope(x, cos, sin_signed):
    lane = lax.broadcasted_iota(jnp.int32, (x.shape[0], 128), 1)
    first_half = (lane % HEAD_DIM) < (HEAD_DIM // 2)
    outs = []
    for j in range(x.shape[1] // 128):
        xc = x[:, j * 128:(j + 1) * 128]
        swapped = jnp.where(first_half, pltpu.roll(xc, 96, 1), pltpu.roll(xc, 32, 1))
        outs.append(xc * cos + swapped * sin_signed)
    return jnp.concatenate(outs, axis=1) if len(outs) > 1 else outs[0]


def _attend(seqs, mask, sink_ref):
    heads = [(n, h) for n in range(len(seqs)) for h in range(N_Q_HEADS)]
    col = lambda x, i: x[:, i * HEAD_DIM:(i + 1) * HEAD_DIM]

    scores = []
    for n, h in heads:
        q = col(seqs[n][0], h).astype(BF16)
        k = col(seqs[n][1], h // Q_PER_KV).astype(BF16)
        s = lax.dot_general(q, k, (((1,), (1,)), ((), ())), preferred_element_type=F32)
        scores.append(jnp.where(mask, s * (HEAD_DIM ** -0.5), NEG_INF))
    probs = []
    for (n, h), s in zip(heads, scores):
        sink = sink_ref[h]
        m = jnp.maximum(jnp.max(s, axis=-1, keepdims=True), sink)
        p = jnp.exp(s - m)
        denom = jnp.sum(p, axis=-1, keepdims=True) + jnp.exp(sink - m)
        probs.append((p / denom).astype(BF16))
    outs = [[] for _ in seqs]
    for (n, h), p in zip(heads, probs):
        outs[n].append(_dot(p, col(seqs[n][2], h // Q_PER_KV).astype(BF16)))
    return [jnp.concatenate(o, axis=1) for o in outs]


def _attn_prompt_kernel(sink_ref, q_ref, kc_ref, kp_ref, vc_ref, vp_ref, cosc_ref, sinc_ref,
                        cosp_ref, sinp_ref, g_ref, init_ref, o_ref, ks_ref, vs_ref):
    del init_ref
    c = pl.program_id(1)
    q = _rope(q_ref[...], cosc_ref[...], sinc_ref[...])
    kc = _rope(kc_ref[...], cosc_ref[...], sinc_ref[...])
    kp = _rope(kp_ref[...], cosp_ref[...], sinp_ref[...])
    kk = jnp.concatenate([kp, kc], axis=0)
    vv = jnp.concatenate([vp_ref[...], vc_ref[...]], axis=0)
    row = lax.broadcasted_iota(jnp.int32, (CHUNK, 2 * CHUNK), 0)
    col = lax.broadcasted_iota(jnp.int32, (CHUNK, 2 * CHUNK), 1)
    mask = (col > row) & (col <= row + WINDOW) & ((col >= CHUNK) | (c > 0))
    (attn,) = _attend([(q, kk, vv)], mask, sink_ref)
    o_ref[...] = (_rms_scale(attn) * g_ref[...]).astype(BF16)
    ks_ref[...] = kc
    vs_ref[...] = vc_ref[...]


def attn_prompt(z, sinks, cos_t, sin_t, mix_g_all, layer):
    blk = lambda b, c: c * BATCH + b
    prev = lambda b, c: jnp.maximum(c - 1, 0) * BATCH + b
    kcol, vcol = D_ATTN // D_KV, D_ATTN // D_KV + 1
    return pl.pallas_call(
        _attn_prompt_kernel,
        grid=(BATCH, N_CHUNKS),
        in_specs=[
            pl.BlockSpec(memory_space=pltpu.SMEM),
            pl.BlockSpec((CHUNK, D_ATTN), lambda b, c: (blk(b, c), 0)),
            pl.BlockSpec((CHUNK, D_KV), lambda b, c: (blk(b, c), kcol)),
            pl.BlockSpec((CHUNK, D_KV), lambda b, c: (prev(b, c), kcol)),
            pl.BlockSpec((CHUNK, D_KV), lambda b, c: (blk(b, c), vcol)),
            pl.BlockSpec((CHUNK, D_KV), lambda b, c: (prev(b, c), vcol)),
            pl.BlockSpec((CHUNK, 128), lambda b, c: (c, 0)),
            pl.BlockSpec((CHUNK, 128), lambda b, c: (c, 0)),
            pl.BlockSpec((CHUNK, 128), lambda b, c: (jnp.maximum(c - 1, 0), 0)),
            pl.BlockSpec((CHUNK, 128), lambda b, c: (jnp.maximum(c - 1, 0), 0)),
            pl.BlockSpec((None, 1, D_ATTN), lambda b, c: (layer, 0, 0)),
            pl.BlockSpec(memory_space=pl.ANY),
        ],
        out_specs=[
            pl.BlockSpec((CHUNK, D_ATTN), lambda b, c: (blk(b, c), 0)),
            pl.BlockSpec((None, CHUNK, D_KV), lambda b, c: (b, 0, 0)),
            pl.BlockSpec((None, CHUNK, D_KV), lambda b, c: (b, 0, 0)),
        ],
        out_shape=[
            jax.ShapeDtypeStruct((T_ALL, D_ATTN), BF16),
            jax.ShapeDtypeStruct((BATCH, CHUNK, D_KV), F32),
            jax.ShapeDtypeStruct((BATCH, CHUNK, D_KV), F32),
        ],
        input_output_aliases={11: 0},
        compiler_params=_params("arbitrary", "arbitrary"),
        name="attn_prompt",
    )(sinks, z, z, z, z, z, cos_t, sin_t, cos_t, sin_t, mix_g_all,
      jnp.zeros((T_ALL, D_ATTN), BF16))


SAMPLE_SEQS_PER_STEP = 4


def _attn_sample_kernel(sink_ref, q_ref, kn_ref, vn_ref, ck_ref, cv_ref, cos_ref, sin_ref, g_ref,
                        prev_ref, o_ref, ks_ref, vs_ref):
    del prev_ref
    row = lax.broadcasted_iota(jnp.int32, (DEC_SEQ, WINDOW + DEC_SEQ), 0)
    col = lax.broadcasted_iota(jnp.int32, (DEC_SEQ, WINDOW + DEC_SEQ), 1)
    mask = (col > row) & (col <= row + WINDOW)
    seqs = []
    for s in range(SAMPLE_SEQS_PER_STEP):
        rows = slice(s * DEC_SEQ, (s + 1) * DEC_SEQ)
        q = _rope(q_ref[rows, :], cos_ref[...], sin_ref[...])
        kn = _rope(kn_ref[rows, :], cos_ref[...], sin_ref[...])
        kk = jnp.concatenate([ck_ref[s], kn], axis=0)
        vv = jnp.concatenate([cv_ref[s], vn_ref[rows, :]], axis=0)
        ks_ref[s] = kk[DEC_SEQ:]
        vs_ref[s] = vv[DEC_SEQ:]
        seqs.append((q, kk, vv))
    normed = [_rms_scale(attn) * g_ref[...] for attn in _attend(seqs, mask, sink_ref)]
    o_ref[...] = jnp.concatenate(normed, axis=0).astype(BF16)


def attn_sample(z, attn_n, cache_k, cache_v, sinks, cos_s, sin_s, mix_g_all, layer):
    nb = SAMPLE_SEQS_PER_STEP
    rows = nb * DEC_SEQ
    base = T_PROMPT // rows
    kcol, vcol = D_ATTN // D_KV, D_ATTN // D_KV + 1
    return pl.pallas_call(
        _attn_sample_kernel,
        grid=(DEC_BATCH // nb,),
        in_specs=[
            pl.BlockSpec(memory_space=pltpu.SMEM),
            pl.BlockSpec((rows, D_ATTN), lambda i: (base + i, 0)),
            pl.BlockSpec((rows, D_KV), lambda i: (base + i, kcol)),
            pl.BlockSpec((rows, D_KV), lambda i: (base + i, vcol)),
            pl.BlockSpec((None, nb, WINDOW, D_KV), lambda i: (layer, i, 0, 0)),
            pl.BlockSpec((None, nb, WINDOW, D_KV), lambda i: (layer, i, 0, 0)),
            pl.BlockSpec((DEC_SEQ, 128), lambda i: (0, 0)),
            pl.BlockSpec((DEC_SEQ, 128), lambda i: (0, 0)),
            pl.BlockSpec((None, 1, D_ATTN), lambda i: (layer, 0, 0)),
            pl.BlockSpec(memory_space=pl.ANY),
        ],
        out_specs=[
            pl.BlockSpec((rows, D_ATTN), lambda i: (base + i, 0)),
            pl.BlockSpec((nb, WINDOW, D_KV), lambda i: (i, 0, 0)),
            pl.BlockSpec((nb, WINDOW, D_KV), lambda i: (i, 0, 0)),
        ],
        out_shape=[
            jax.ShapeDtypeStruct((T_ALL, D_ATTN), BF16),
            jax.ShapeDtypeStruct((DEC_BATCH, WINDOW, D_KV), F32),
            jax.ShapeDtypeStruct((DEC_BATCH, WINDOW, D_KV), F32),
        ],
        input_output_aliases={9: 0},
        compiler_params=_params("arbitrary"),
        name="attn_sample",
    )(sinks, z, z, z, cache_k, cache_v, cos_s, sin_s, mix_g_all, attn_n)


def _sgu_kernel(u_ref, v_ref, w_ref, bias_ref, lng_ref, lnb_ref, g_ref, o_ref, sv_ref):
    u = _gelu(u_ref[...])
    v = _gelu(v_ref[...])
    vc = v - jnp.mean(v, axis=-1, keepdims=True)
    v = vc * lax.rsqrt(jnp.mean(vc * vc, axis=-1, keepdims=True) + NORM_EPS)
    v = v * lng_ref[...] + lnb_ref[...]
    sv_ref[...] = v
    row = lax.broadcasted_iota(jnp.int32, (CHUNK, CHUNK), 0)
    col = lax.broadcasted_iota(jnp.int32, (CHUNK, CHUNK), 1)
    causal = col <= row
    vb = v.astype(BF16)
    mixed = []
    for h in range(N_SGU_HEADS):
        w = jnp.where(causal, w_ref[h], 0.0).astype(BF16)
        mixed.append(_dot(w, vb[:, h * SGU_HEAD_DIM:(h + 1) * SGU_HEAD_DIM]))
    sgu = u * (jnp.concatenate(mixed, axis=1) + bias_ref[...])
    o_ref[...] = (_rms_scale(sgu) * g_ref[...]).astype(BF16)


def sgu(z, w_mix, bias_mix, ln_g_all, ln_b_all, mix_g_all, layer):
    n_prompt = T_PROMPT // CHUNK
    kind = lambda i: jnp.where(i >= n_prompt, 1, 0)
    ucol, vcol = (D_ATTN + 2 * D_KV) // D_SGU, (D_ATTN + 2 * D_KV) // D_SGU + 1
    gcol = D_ATTN // D_SGU
    return pl.pallas_call(
        _sgu_kernel,
        grid=(T_ALL // CHUNK,),
        in_specs=[
            pl.BlockSpec((CHUNK, D_SGU), lambda i: (i, ucol)),
            pl.BlockSpec((CHUNK, D_SGU), lambda i: (i, vcol)),
            pl.BlockSpec((None, None, N_SGU_HEADS, CHUNK, CHUNK), lambda i: (layer, kind(i), 0, 0, 0)),
            pl.BlockSpec((None, None, CHUNK, D_SGU), lambda i: (layer, kind(i), 0, 0)),
            pl.BlockSpec((None, 1, D_SGU), lambda i: (layer, 0, 0)),
            pl.BlockSpec((None, 1, D_SGU), lambda i: (layer, 0, 0)),
            pl.BlockSpec((None, 1, D_SGU), lambda i: (layer, 0, gcol)),
        ],
        out_specs=[
            pl.BlockSpec((CHUNK, D_SGU), lambda i: (i, 0)),
            pl.BlockSpec((CHUNK, D_SGU), lambda i: (jnp.maximum(i - n_prompt, 0), 0)),
        ],
        out_shape=[
            jax.ShapeDtypeStruct((T_ALL, D_SGU), BF16),
            jax.ShapeDtypeStruct((T_SAMPLE, D_SGU), F32),
        ],
        compiler_params=_params("arbitrary"),
        name="sgu",
    )(z, z, w_mix, bias_mix, ln_g_all, ln_b_all, mix_g_all)


def _ssm_disc_kernel(lr_ref, li_ref, ldt_ref, br_ref, bi_ref, ar_ref, ai_ref, bbr_ref, bbi_ref):
    lr = lr_ref[...]
    li = li_ref[...]
    dt = jnp.exp(ldt_ref[...])
    mag = jnp.exp(lr * dt)
    a_re = mag * jnp.cos(li * dt)
    a_im = mag * jnp.sin(li * dt)
    nr, ni = a_re - 1.0, a_im
    den = lr * lr + li * li
    c_re = (nr * lr + ni * li) / den
    c_im = (ni * lr - nr * li) / den
    ar_ref[...] = a_re
    ai_ref[...] = a_im
    bbr_ref[...] = c_re * br_ref[...] - c_im * bi_ref[...]
    bbi_ref[...] = c_re * bi_ref[...] + c_im * br_ref[...]


def ssm_discretize(lam_re, lam_im, log_dt, b_re_t, b_im_t):
    d, g, p = lam_re.shape
    c = b_re_t.shape[2]
    return pl.pallas_call(
        _ssm_disc_kernel,
        out_shape=[
            jax.ShapeDtypeStruct((d, g, 1, p), F32),
            jax.ShapeDtypeStruct((d, g, 1, p), F32),
            jax.ShapeDtypeStruct((d, g, c, p), F32),
            jax.ShapeDtypeStruct((d, g, c, p), F32),
        ],
        compiler_params=pltpu.CompilerParams(vmem_limit_bytes=VMEM_LIMIT_BYTES),
        name="ssm_discretize",
    )(lam_re.reshape(d, g, 1, p), lam_im.reshape(d, g, 1, p), log_dt.reshape(d, g, 1, 1),
      b_re_t, b_im_t)


SCAN_COLS = 512
SCAN_BLOCKS = SCAN_COLS // 128


def _ssm_kernel(u_ref, bre_ref, bim_ref, cre_ref, cim_ref, are_ref, aim_ref, d_ref, gw_ref, gb_ref,
                g_ref, h0r_ref, h0i_ref, prev_ref, o_ref, hr_ref, hi_ref, xr_ref, xi_ref, *, ns, lc):
    del prev_ref
    half_u = D_SSM // 2
    half_s = N_STATE // 2

    @pl.when(pl.program_id(0) == 0)
    def _():
        hr_ref[...] = h0r_ref[...]
        hi_ref[...] = h0i_ref[...]

    u = u_ref[...]
    ub = u.astype(BF16)
    blocks_per_half = half_s // 128
    for hf in range(2):
        uh = ub[:, hf * half_u:(hf + 1) * half_u]
        x_re = _dot(uh, bre_ref[hf].astype(BF16))
        x_im = _dot(uh, bim_ref[hf].astype(BF16))
        for j in range(blocks_per_half):
            xr_ref[hf * blocks_per_half + j] = x_re[:, j * 128:(j + 1) * 128]
            xi_ref[hf * blocks_per_half + j] = x_im[:, j * 128:(j + 1) * 128]

    for cc in range(N_STATE // SCAN_COLS):
        blocks = range(cc * SCAN_BLOCKS, (cc + 1) * SCAN_BLOCKS)
        a_re = [jnp.broadcast_to(are_ref[:, j * 128:(j + 1) * 128], (ns, 128)) for j in blocks]
        a_im = [jnp.broadcast_to(aim_ref[:, j * 128:(j + 1) * 128], (ns, 128)) for j in blocks]

        def step(t, carry):
            rows = pl.ds(t, ns, stride=lc)
            new = []
            for n, j in enumerate(blocks):
                h_re, h_im = carry[n]
                n_re = a_re[n] * h_re - a_im[n] * h_im + xr_ref[j, rows, :]
                n_im = a_re[n] * h_im + a_im[n] * h_re + xi_ref[j, rows, :]
                xr_ref[j, rows, :] = n_re
                xi_ref[j, rows, :] = n_im
                new.append((n_re, n_im))
            return tuple(new)

        init = tuple((hr_ref[:, j * 128:(j + 1) * 128], hi_ref[:, j * 128:(j + 1) * 128]) for j in blocks)
        final = lax.fori_loop(0, lc, step, init)
        for n, j in enumerate(blocks):
            hr_ref[:, j * 128:(j + 1) * 128] = final[n][0]
            hi_ref[:, j * 128:(j + 1) * 128] = final[n][1]

    ys = []
    for hf in range(2):
        js = range(hf * blocks_per_half, (hf + 1) * blocks_per_half)
        h_re = jnp.concatenate([xr_ref[j].astype(BF16) for j in js], axis=1)
        h_im = jnp.concatenate([xi_ref[j].astype(BF16) for j in js], axis=1)
        ys.append(_dot(h_re, cre_ref[hf].astype(BF16)) - _dot(h_im, cim_ref[hf].astype(BF16)))
    y = jnp.concatenate(ys, axis=1) + d_ref[...] * u
    g = _gelu(y)
    gate = _dot(g.astype(BF16), gw_ref[...].astype(BF16)) + gb_ref[...]
    ssm = g * jax.nn.sigmoid(gate)
    o_ref[...] = (_rms_scale(ssm) * g_ref[...]).astype(BF16)


def ssm(z, prev, h0_re, h0_im, bre, bim, cre, cim, a_re, a_im, d_all, glu_w_all, glu_b_all,
        mix_g_all, layer, *, ns, lc, first_block, n_blocks):
    rows = ns * lc
    zcol = (D_ATTN + 2 * D_KV + 2 * D_SGU) // D_SSM
    gcol = (D_ATTN + D_SGU) // D_SSM
    full = lambda shape: pl.BlockSpec(shape, lambda i: (0,) * len(shape))
    lay3 = lambda shape: pl.BlockSpec((None,) + shape, lambda i: (layer,) + (0,) * len(shape))
    in_specs = [
        pl.BlockSpec((rows, D_SSM), lambda i: (first_block + i, zcol)),
        lay3((2, D_SSM // 2, N_STATE // 2)),
        lay3((2, D_SSM // 2, N_STATE // 2)),
        lay3((2, N_STATE // 2, D_SSM // 2)),
        lay3((2, N_STATE // 2, D_SSM // 2)),
        lay3((1, N_STATE)),
        lay3((1, N_STATE)),
        lay3((1, D_SSM)),
        lay3((D_SSM, D_SSM)),
        lay3((1, D_SSM)),
        pl.BlockSpec((None, 1, D_SSM), lambda i: (layer, 0, gcol)),
        full((ns, N_STATE)),
        full((ns, N_STATE)),
        pl.BlockSpec(memory_space=pl.ANY),
    ]
    args = [z, bre, bim, cre, cim, a_re, a_im, d_all, glu_w_all, glu_b_all, mix_g_all, h0_re, h0_im,
            prev]
    return pl.pallas_call(
        functools.partial(_ssm_kernel, ns=ns, lc=lc),
        grid=(n_blocks,),
        in_specs=in_specs,
        out_specs=[
            pl.BlockSpec((rows, D_SSM), lambda i: (first_block + i, 0)),
            full((ns, N_STATE)),
            full((ns, N_STATE)),
        ],
        out_shape=[
            jax.ShapeDtypeStruct((T_ALL, D_SSM), BF16),
            jax.ShapeDtypeStruct((ns, N_STATE), F32),
            jax.ShapeDtypeStruct((ns, N_STATE), F32),
        ],
        scratch_shapes=[pltpu.VMEM((N_STATE // 128, rows, 128), F32),
                        pltpu.VMEM((N_STATE // 128, rows, 128), F32)],
        input_output_aliases={13: 0},
        compiler_params=_params("arbitrary"),
        name="ssm",
    )(*args)


def _out_proj_kernel(a0_ref, a1_ref, a2_ref, w_ref, r_ref, o_ref, *, tn):
    k0 = a0_ref.shape[1]
    k1 = k0 + a1_ref.shape[1]
    for j in range(o_ref.shape[1] // tn):
        cols = slice(j * tn, (j + 1) * tn)
        acc = _dot(a0_ref[...], w_ref[:k0, cols])
        acc += _dot(a1_ref[...], w_ref[k0:k1, cols])
        acc += _dot(a2_ref[...], w_ref[k1:, cols])
        o_ref[:, cols] = r_ref[:, cols] + acc


def out_proj(a0, a1, a2, w_all, resid, layer, *, tm, tn):
    t, n = resid.shape
    k = w_all.shape[1]
    return pl.pallas_call(
        functools.partial(_out_proj_kernel, tn=tn),
        grid=(t // tm,),
        in_specs=[
            pl.BlockSpec((tm, a0.shape[1]), lambda i: (i, 0)),
            pl.BlockSpec((tm, a1.shape[1]), lambda i: (i, 0)),
            pl.BlockSpec((tm, a2.shape[1]), lambda i: (i, 0)),
            _resident((None, k, n), lambda i: (layer, 0, 0)),
            pl.BlockSpec((tm, n), lambda i: (i, 0)),
        ],
        out_specs=pl.BlockSpec((tm, n), lambda i: (i, 0)),
        out_shape=jax.ShapeDtypeStruct((t, n), F32),
        compiler_params=_params("arbitrary"),
        name="out_proj",
    )(a0, a1, a2, w_all, resid)


def _norm_kernel(x_ref, g_ref, o_ref):
    o_ref[...] = (_rms_scale(x_ref[...]) * g_ref[...]).astype(o_ref.dtype)


def norm_rows(x, g_all, layer, *, tm, dtype):
    t, k = x.shape
    return pl.pallas_call(
        _norm_kernel,
        grid=(t // tm,),
        in_specs=[pl.BlockSpec((tm, k), lambda i: (i, 0)),
                  pl.BlockSpec((None, 1, k), lambda i: (layer, 0, 0))],
        out_specs=pl.BlockSpec((tm, k), lambda i: (i, 0)),
        out_shape=jax.ShapeDtypeStruct((t, k), dtype),
        compiler_params=_params("arbitrary"),
        name="norm_rows",
    )(x, g_all)


def _router_kernel(x_ref, g_ref, w_ref, idx_ref, p_ref):
    h = (_rms_scale(x_ref[...]) * g_ref[...]).astype(BF16)
    logits = _dot(h, w_ref[...].astype(BF16))
    e = jnp.exp(logits - jnp.max(logits, axis=-1, keepdims=True))
    probs = e / jnp.sum(e, axis=-1, keepdims=True)
    lane = lax.broadcasted_iota(jnp.int32, probs.shape, 1)
    p1 = jnp.max(probs, axis=-1, keepdims=True)
    i1 = jnp.min(jnp.where(probs == p1, lane, N_EXPERTS), axis=-1, keepdims=True)
    rest = jnp.where(lane == i1, -1.0, probs)
    p2 = jnp.max(rest, axis=-1, keepdims=True)
    i2 = jnp.min(jnp.where(rest == p2, lane, N_EXPERTS), axis=-1, keepdims=True)
    tot = p1 + p2
    idx_ref[...] = jnp.where(lane == 0, i1, jnp.where(lane == 1, i2, 0))
    p_ref[...] = jnp.where(lane == 0, p1 / tot, jnp.where(lane == 1, p2 / tot, 0.0))


def router_top2(x, g_all, w_all, layer, moe_layer, *, tm):
    t, k = x.shape
    return pl.pallas_call(
        _router_kernel,
        grid=(t // tm,),
        in_specs=[pl.BlockSpec((tm, k), lambda i: (i, 0)),
                  pl.BlockSpec((None, 1, k), lambda i: (layer, 0, 0)),
                  pl.BlockSpec((None, k, N_EXPERTS), lambda i: (moe_layer, 0, 0))],
        out_specs=[pl.BlockSpec((tm, N_EXPERTS), lambda i: (i, 0)),
                   pl.BlockSpec((tm, N_EXPERTS), lambda i: (i, 0))],
        out_shape=[jax.ShapeDtypeStruct((t, N_EXPERTS), jnp.int32),
                   jax.ShapeDtypeStruct((t, N_EXPERTS), F32)],
        compiler_params=_params("arbitrary"),
        name="router_top2",
    )(x, g_all, w_all)


def _row_copy(src_hbm, row, dst, r, sem):
    return pltpu.make_async_copy(src_hbm.at[pl.ds(row, 1)], dst.at[pl.ds(r, 1)], sem)


def _gather_norm_kernel(idx_ref, act_ref, x_hbm, g_ref, o_ref, buf, sems):
    tg = buf.shape[1]
    i = pl.program_id(0)
    n = pl.num_programs(0)

    def copies(tile, fn):
        slot = tile % 2
        base = tile * tg

        def body(r, _):
            fn(_row_copy(x_hbm, idx_ref[base + r], buf.at[slot], r, sems.at[slot]))
            return 0

        lax.fori_loop(0, tg, body, 0, unroll=DMA_ISSUE_UNROLL)

    @pl.when(jnp.logical_and(i == 0, act_ref[0] > 0))
    def _():
        copies(0, lambda cp: cp.start())

    @pl.when(jnp.logical_and(i + 1 < n, act_ref[jnp.minimum(i + 1, n - 1)] > 0))
    def _():
        copies(i + 1, lambda cp: cp.start())

    @pl.when(act_ref[i] > 0)
    def _():
        copies(i, lambda cp: cp.wait())
        o_ref[...] = (_rms_scale(buf[i % 2]) * g_ref[...]).astype(BF16)

    @pl.when(act_ref[i] == 0)
    def _():
        o_ref[...] = jnp.zeros_like(o_ref)


def gather_norm(x, idx, tile_active, g_all, layer, *, tg):
    n = idx.shape[0]
    k = x.shape[1]
    return pl.pallas_call(
        _gather_norm_kernel,
        grid_spec=pltpu.PrefetchScalarGridSpec(
            num_scalar_prefetch=2,
            grid=(n // tg,),
            in_specs=[pl.BlockSpec(memory_space=pl.ANY),
                      pl.BlockSpec((None, 1, k), lambda i, idx, act: (layer, 0, 0))],
            out_specs=pl.BlockSpec((tg, k), lambda i, idx, act: (i, 0)),
            scratch_shapes=[pltpu.VMEM((2, tg, k), F32), pltpu.SemaphoreType.DMA((2,))],
        ),
        out_shape=jax.ShapeDtypeStruct((n, k), BF16),
        compiler_params=_params("arbitrary"),
        name="gather_norm",
    )(idx, tile_active, x, g_all)


def _combine_kernel(slot_ref, o_hbm, x_ref, out_ref, buf_a, buf_b, sems):
    tc = buf_a.shape[1]
    i = pl.program_id(0)
    n = pl.num_programs(0)

    def copies(tile, fn):
        slot = tile % 2
        base = tile * tc

        def body(r, _):
            fn(_row_copy(o_hbm, slot_ref[2 * (base + r)], buf_a.at[slot], r, sems.at[slot]))
            fn(_row_copy(o_hbm, slot_ref[2 * (base + r) + 1], buf_b.at[slot], r, sems.at[slot]))
            return 0

        lax.fori_loop(0, tc, body, 0, unroll=DMA_ISSUE_UNROLL)

    @pl.when(i == 0)
    def _():
        copies(0, lambda cp: cp.start())

    @pl.when(i + 1 < n)
    def _():
        copies(i + 1, lambda cp: cp.start())

    copies(i, lambda cp: cp.wait())
    out_ref[...] = x_ref[...] + (buf_a[i % 2] + buf_b[i % 2])


def moe_combine(o_sorted, slots, x, *, tc):
    t, k = x.shape
    return pl.pallas_call(
        _combine_kernel,
        grid_spec=pltpu.PrefetchScalarGridSpec(
            num_scalar_prefetch=1,
            grid=(t // tc,),
            in_specs=[pl.BlockSpec(memory_space=pl.ANY),
                      pl.BlockSpec((tc, k), lambda i, s: (i, 0))],
            out_specs=pl.BlockSpec((tc, k), lambda i, s: (i, 0)),
            scratch_shapes=[pltpu.VMEM((2, tc, k), F32), pltpu.VMEM((2, tc, k), F32),
                            pltpu.SemaphoreType.DMA((2,))],
        ),
        out_shape=jax.ShapeDtypeStruct((t, k), F32),
        compiler_params=_params("arbitrary"),
        name="moe_combine",
    )(slots, o_sorted, x)


def _ffn_kernel(te_ref, nsub_ref, x_ref, wg_ref, wu_ref, wd_ref, s_ref, *rest, nf, rb, has_resid):
    if has_resid:
        r_ref, o_ref = rest
    else:
        (o_ref,) = rest
    f = pl.program_id(1)

    @pl.when(f == 0)
    def _():
        o_ref[...] = jnp.zeros_like(o_ref)

    def row_block(r, carry):
        rows = pl.ds(pl.multiple_of(r * rb, rb), rb)
        x = x_ref[rows, :]
        gate = _dot(x, wg_ref[...].astype(BF16))
        up = _dot(x, wu_ref[...].astype(BF16))
        h = (gate * jax.nn.sigmoid(gate) * up).astype(BF16)
        o_ref[rows, :] += _dot(h, wd_ref[...].astype(BF16))
        return carry

    lax.fori_loop(0, nsub_ref[pl.program_id(0)], row_block, 0)

    @pl.when(f == nf - 1)
    def _():
        out = o_ref[...] * s_ref[...]
        if has_resid:
            out = r_ref[...] + out
        o_ref[...] = out


def grouped_ffn(x_rows, tile_expert, tile_nsub, scale, wg_all, wu_all, wd_all, lead, resid,
                *, tm, tf, rb):
    s, k = x_rows.shape
    ff = wg_all.shape[-1]
    nf = ff // tf
    n_tiles = s // tm
    nlead = len(lead)
    fblk = lambda i, f, act: jnp.where(act[i] > 0, f, nf - 1)
    w_in_spec = pl.BlockSpec((None,) * (nlead + 1) + (k, tf),
                             lambda i, f, te, act: lead + (te[i], 0, fblk(i, f, act)))
    w_out_spec = pl.BlockSpec((None,) * (nlead + 1) + (tf, k),
                              lambda i, f, te, act: lead + (te[i], fblk(i, f, act), 0))
    row_spec = lambda width: pl.BlockSpec((tm, width), lambda i, f, te, act: (i, 0))
    in_specs = [row_spec(k), w_in_spec, w_in_spec, w_out_spec, row_spec(1)]
    args = [x_rows, wg_all, wu_all, wd_all, scale]
    if resid is not None:
        in_specs.append(row_spec(k))
        args.append(resid)
    return pl.pallas_call(
        functools.partial(_ffn_kernel, nf=nf, rb=rb, has_resid=resid is not None),
        grid_spec=pltpu.PrefetchScalarGridSpec(
            num_scalar_prefetch=2,
            grid=(n_tiles, nf),
            in_specs=in_specs,
            out_specs=row_spec(k),
        ),
        out_shape=jax.ShapeDtypeStruct((s, k), F32),
        compiler_params=_params("arbitrary", "arbitrary"),
        name="grouped_ffn",
    )(tile_expert, tile_nsub, *args)


def _moe_plan(top_i, top_p, tm, rb, tg, n_tiles):
    t = top_i.shape[0]
    e = top_i.reshape(-1)
    p = top_p.reshape(-1)
    onehot = (e[:, None] == jnp.arange(N_EXPERTS, dtype=jnp.int32)[None, :]).astype(jnp.int32)
    csum = jnp.cumsum(onehot, axis=0)
    rank = jnp.take_along_axis(csum, e[:, None], axis=1)[:, 0] - 1
    counts = csum[-1]
    tiles_e = (counts + tm - 1) // tm
    tile_end = jnp.cumsum(tiles_e)
    tile_start = tile_end - tiles_e
    dest = tile_start[e] * tm + rank
    src_entry = jnp.full((n_tiles * tm,), -1, jnp.int32).at[dest].set(
        jnp.arange(2 * t, dtype=jnp.int32), unique_indices=True)
    entry = jnp.maximum(src_entry, 0)
    src_token = entry // 2
    scale = jnp.where(src_entry >= 0, p[entry], 0.0)
    tile_ids = jnp.arange(n_tiles, dtype=jnp.int32)
    tile_expert = jnp.sum((tile_ids[:, None] >= tile_end[None, :]).astype(jnp.int32), axis=1)
    active = (tile_ids < tile_end[-1]).astype(jnp.int32)
    last_expert = jnp.minimum(tile_expert[jnp.maximum(tile_end[-1] - 1, 0)], N_EXPERTS - 1)
    tile_expert = jnp.where(active > 0, jnp.minimum(tile_expert, N_EXPERTS - 1), last_expert)
    tile_valid = jnp.clip(counts[tile_expert] - (tile_ids - tile_start[tile_expert]) * tm, 0, tm) * active
    tile_nsub = (tile_valid + rb - 1) // rb
    per = tm // tg
    g_ids = jnp.arange(n_tiles * per, dtype=jnp.int32)
    g_active = (tile_valid[g_ids // per] > (g_ids % per) * tg).astype(jnp.int32)
    return (src_token, scale.reshape(-1, 1), dest.astype(jnp.int32), tile_expert.astype(jnp.int32),
            tile_nsub.astype(jnp.int32), g_active)


def _final_norm_kernel(x_ref, g_ref, op_ref, os_ref):
    i = pl.program_id(0)
    y = _rms_scale(x_ref[...]) * g_ref[...]

    @pl.when(i < T_PROMPT // CHUNK)
    def _():
        op_ref[...] = y

    @pl.when(i >= T_PROMPT // CHUNK)
    def _():
        os_ref[...] = y


def final_norm(x, g):
    n_prompt = T_PROMPT // CHUNK
    k = x.shape[1]

    def prompt_block(i):
        j = jnp.minimum(i, n_prompt - 1)
        return ((j % BATCH) * N_CHUNKS + j // BATCH, 0)

    return pl.pallas_call(
        _final_norm_kernel,
        grid=(T_ALL // CHUNK,),
        in_specs=[pl.BlockSpec((CHUNK, k), lambda i: (i, 0)),
                  pl.BlockSpec((1, k), lambda i: (0, 0))],
        out_specs=[pl.BlockSpec((CHUNK, k), prompt_block),
                   pl.BlockSpec((CHUNK, k), lambda i: (jnp.maximum(i - n_prompt, 0), 0))],
        out_shape=[jax.ShapeDtypeStruct((T_PROMPT, k), F32),
                   jax.ShapeDtypeStruct((T_SAMPLE, k), F32)],
        compiler_params=_params("arbitrary"),
        name="final_norm",
    )(x, g.reshape(1, k))


def _rope_tables(pos):
    half = HEAD_DIM // 2
    inv_freq = jnp.power(ROPE_THETA, -jnp.arange(half, dtype=F32) / half)
    ang = pos.astype(F32)[:, None] * inv_freq[None, :]
    cos, sin = jnp.cos(ang), jnp.sin(ang)
    return jnp.tile(cos, (1, 4)), jnp.tile(jnp.concatenate([-sin, sin], axis=1), (1, 2))


def _block_diag(blocks):
    *lead, n, r, c = blocks.shape
    eye = jnp.eye(n, dtype=blocks.dtype)
    return (blocks[..., :, :, None, :] * eye[:, None, :, None]).reshape(*lead, n * r, n * c)


def kernel(x_prompt, x_sample, cache_k, cache_v, state_ssm_re, state_ssm_im, norm1_g, w_in, attn_sinks, sgu_ln_g, sgu_ln_b, sgu_w, sgu_b, ssm_lambda_re, ssm_lambda_im, ssm_log_dt, ssm_b_re, ssm_b_im, ssm_c_re, ssm_c_im, ssm_d, ssm_glu_w, ssm_glu_b, mix_norm_g, w_out, norm2_g, ffn_w_gate, ffn_w_up, ffn_w_down, moe_router, moe_w_gate, moe_w_up, moe_w_down, final_norm_g):
    d = D_MODEL
    xp = x_prompt.reshape(BATCH, N_CHUNKS, CHUNK, d).transpose(1, 0, 2, 3).reshape(T_PROMPT, d)
    x = jnp.concatenate([xp, x_sample.reshape(T_SAMPLE, d)], axis=0)

    cos_p, sin_p = _rope_tables(jnp.arange(SEQ, dtype=jnp.int32))
    cos_s, sin_s = _rope_tables(PAST_LEN + jnp.arange(DEC_SEQ, dtype=jnp.int32))

    row3 = lambda a: a.reshape(DEPTH, 1, a.shape[-1])
    norm1_g3, norm2_g3, mix_g3 = row3(norm1_g), row3(norm2_g), row3(mix_norm_g)
    ln_g3, ln_b3 = row3(sgu_ln_g), row3(sgu_ln_b)
    glu_b3 = row3(ssm_glu_b)
    ssm_d3 = ssm_d.reshape(DEPTH, 1, D_SSM)
    cache_k4 = cache_k.reshape(DEPTH, DEC_BATCH, WINDOW, D_KV)
    cache_v4 = cache_v.reshape(DEPTH, DEC_BATCH, WINDOW, D_KV)
    w_in_b, w_out_b = w_in.astype(BF16), w_out.astype(BF16)

    reps = CHUNK // DEC_SEQ
    w_small = jnp.tril(sgu_w[:, :, :DEC_SEQ, :DEC_SEQ])
    eye = jnp.eye(reps, dtype=F32)
    w_sample = (eye[None, None, :, None, :, None] * w_small[:, :, None, :, None, :]).reshape(
        DEPTH, N_SGU_HEADS, CHUNK, CHUNK)
    w_mix = jnp.stack([sgu_w, w_sample], axis=1)
    b_prompt = jnp.repeat(sgu_b.transpose(0, 2, 1), SGU_HEAD_DIM, axis=2)
    b_sample = jnp.tile(b_prompt[:, :DEC_SEQ], (1, reps, 1))
    bias_mix = jnp.stack([b_prompt, b_sample], axis=1)

    a_re, a_im, bbt_re, bbt_im = ssm_discretize(
        ssm_lambda_re, ssm_lambda_im, ssm_log_dt,
        ssm_b_re.transpose(0, 1, 3, 2), ssm_b_im.transpose(0, 1, 3, 2))
    a_re3 = a_re.reshape(DEPTH, 1, N_STATE)
    a_im3 = a_im.reshape(DEPTH, 1, N_STATE)
    hg = N_SSM_GROUPS // 2
    halves = lambda blocks: _block_diag(blocks.reshape(DEPTH, 2, hg, *blocks.shape[2:]))
    bre, bim = halves(bbt_re), halves(bbt_im)
    cre = halves(ssm_c_re.transpose(0, 1, 3, 2))
    cim = halves(ssm_c_im.transpose(0, 1, 3, 2))
    zeros_state = jnp.zeros((BATCH, N_STATE), F32)

    ones_tiles = jnp.ones((T_ALL // ROW_TILE,), jnp.int32)
    ones_scale = jnp.ones((T_ALL, 1), F32)

    outs = {k: [] for k in ("kp", "vp", "hrp", "hip", "ks", "vs", "hrs", "his", "sgs")}
    for l in range(DEPTH):
        z = norm_matmul(x, norm1_g3, w_in_b, l, tm=ROW_TILE, tn=512)
        sinks = attn_sinks[l]
        attn_n, k_p, v_p = attn_prompt(z, sinks, cos_p, sin_p, mix_g3, l)
        attn_n, k_s, v_s = attn_sample(z, attn_n, cache_k4, cache_v4, sinks, cos_s, sin_s, mix_g3, l)
        sgu_n, sv_s = sgu(z, w_mix, bias_mix, ln_g3, ln_b3, mix_g3, l)
        ssm_args = (bre, bim, cre, cim, a_re3, a_im3, ssm_d3, ssm_glu_w, glu_b3, mix_g3, l)
        ssm_n, hr_p, hi_p = ssm(z, jnp.zeros((T_ALL, D_SSM), BF16), zeros_state, zeros_state, *ssm_args,
                                ns=BATCH, lc=CHUNK, first_block=0, n_blocks=N_CHUNKS)
        ssm_n, hr_s, hi_s = ssm(z, ssm_n, state_ssm_re[l].reshape(DEC_BATCH, N_STATE),
                                state_ssm_im[l].reshape(DEC_BATCH, N_STATE), *ssm_args,
                                ns=DEC_BATCH, lc=DEC_SEQ, first_block=T_PROMPT // T_SAMPLE, n_blocks=1)
        x = out_proj(attn_n, sgu_n, ssm_n, w_out_b, x, l, tm=ROW_TILE, tn=512)

        outs["kp"].append(k_p.reshape(BATCH, WINDOW, N_KV_HEADS, HEAD_DIM))
        outs["vp"].append(v_p.reshape(BATCH, WINDOW, N_KV_HEADS, HEAD_DIM))
        outs["hrp"].append(hr_p.reshape(BATCH, N_SSM_GROUPS, SSM_STATE))
        outs["hip"].append(hi_p.reshape(BATCH, N_SSM_GROUPS, SSM_STATE))
        outs["ks"].append(k_s.reshape(DEC_BATCH, WINDOW, N_KV_HEADS, HEAD_DIM))
        outs["vs"].append(v_s.reshape(DEC_BATCH, WINDOW, N_KV_HEADS, HEAD_DIM))
        outs["hrs"].append(hr_s.reshape(DEC_BATCH, N_SSM_GROUPS, SSM_STATE))
        outs["his"].append(hi_s.reshape(DEC_BATCH, N_SSM_GROUPS, SSM_STATE))
        outs["sgs"].append(sv_s.reshape(DEC_BATCH, DEC_SEQ, D_SGU))

        i = l // 2
        if l % 2 == 0:
            h2 = norm_rows(x, norm2_g3, l, tm=ROW_TILE, dtype=BF16)
            x = grouped_ffn(h2, jnp.full_like(ones_tiles, i), ones_tiles * (ROW_TILE // DENSE_ROW_BLOCK),
                            ones_scale, ffn_w_gate, ffn_w_up, ffn_w_down, (), x,
                            tm=ROW_TILE, tf=FF_TILE, rb=DENSE_ROW_BLOCK)
        else:
            top_i, top_p = router_top2(x, norm2_g3, moe_router, l, i, tm=ROW_TILE)
            src_token, scale, slots, tile_expert, tile_nsub, gather_active = _moe_plan(
                top_i[:, :2], top_p[:, :2], MOE_ROW_TILE, MOE_ROW_BLOCK, GATHER_TILE, MOE_TILES)
            xs = gather_norm(x, src_token, gather_active, norm2_g3, l, tg=GATHER_TILE)
            o_sorted = grouped_ffn(xs, tile_expert, tile_nsub, scale,
                                   moe_w_gate, moe_w_up, moe_w_down, (i,), None,
                                   tm=MOE_ROW_TILE, tf=FF_TILE, rb=MOE_ROW_BLOCK)
            x = moe_combine(o_sorted, slots, x, tc=GATHER_TILE)

    y_prompt, y_sample = final_norm(x, final_norm_g)
    st = lambda key: jnp.stack(outs[key])
    return (y_prompt.reshape(BATCH, SEQ, d), y_sample.reshape(DEC_BATCH, DEC_SEQ, d),
            st("kp"), st("vp"), st("hrp"), st("hip"), st("ks"), st("vs"), st("hrs"), st("his"), st("sgs"))
```

```python
import functools
import math

import jax
import jax.numpy as jnp
from jax import lax
from jax.experimental import pallas as pl
from jax.experimental.pallas import tpu as pltpu

F32 = jnp.float32
BF16 = jnp.bfloat16

D_MODEL = 2048
BATCH = 4
SEQ = 2048
DEPTH = 4
DEC_BATCH = 32
DEC_SEQ = 8
PAST_LEN = 16384
HEAD_DIM = 64
D_ATTN = 1024
N_Q_HEADS = 16
N_KV_HEADS = 4
Q_PER_KV = 4
D_KV = 256
WINDOW = 128
ROPE_THETA = 10000.0
D_SGU = 512
N_SGU_HEADS = 4
SGU_HEAD_DIM = 128
D_SSM = 512
SSM_GROUP = 16
N_SSM_GROUPS = 32
SSM_STATE = 64
N_STATE = N_SSM_GROUPS * SSM_STATE
D_IN_PROJ = 3072
N_EXPERTS = 8
NORM_EPS = 1e-6
NEG_INF = -1e30
SQRT_HALF = math.sqrt(0.5)

CHUNK = 128
N_CHUNKS = SEQ // CHUNK
T_PROMPT = BATCH * SEQ
T_SAMPLE = DEC_BATCH * DEC_SEQ
T_ALL = T_PROMPT + T_SAMPLE

VMEM_LIMIT_BYTES = 56 * 1024 * 1024

ROW_TILE = 768
DENSE_ROW_BLOCK = ROW_TILE
MOE_ROW_TILE = 1152
MOE_ROW_BLOCK = MOE_ROW_TILE // 2
MOE_TILES = (2 * T_ALL) // MOE_ROW_TILE + N_EXPERTS
FF_TILE = 256
GATHER_TILE = 384
DMA_ISSUE_UNROLL = 8


def _params(*sem):
    return pltpu.CompilerParams(dimension_semantics=sem, vmem_limit_bytes=VMEM_LIMIT_BYTES)


def _rms_scale(x):
    return x * lax.rsqrt(jnp.mean(x * x, axis=-1, keepdims=True) + NORM_EPS)


def _gelu(x):
    return 0.5 * x * (1.0 + lax.erf(x * SQRT_HALF))


def _dot(a, b):
    return jnp.dot(a, b, preferred_element_type=F32)


def _norm_matmul_kernel(x_ref, g_ref, w_ref, o_ref, *, tn):
    h = (_rms_scale(x_ref[...]) * g_ref[...]).astype(BF16)
    for j in range(o_ref.shape[1] // tn):
        o_ref[:, j * tn:(j + 1) * tn] = _dot(h, w_ref[:, j * tn:(j + 1) * tn])


def _resident(block_shape, index_map):
    return pl.BlockSpec(block_shape, index_map, pipeline_mode=pl.Buffered(1))


def norm_matmul(x, g_all, w_all, layer, *, tm, tn):
    t, k = x.shape
    n = w_all.shape[-1]
    return pl.pallas_call(
        functools.partial(_norm_matmul_kernel, tn=tn),
        grid=(t // tm,),
        in_specs=[
            pl.BlockSpec((tm, k), lambda i: (i, 0)),
            pl.BlockSpec((None, 1, k), lambda i: (layer, 0, 0)),
            _resident((None, k, n), lambda i: (layer, 0, 0)),
        ],
        out_specs=pl.BlockSpec((tm, n), lambda i: (i, 0)),
        out_shape=jax.ShapeDtypeStruct((t, n), F32),
        compiler_params=_params("arbitrary"),
        name="norm_matmul",
    )(x, g_all, w_all)


def _rope(x, cos, sin_signed):
    lane = lax.broadcasted_iota(jnp.int32, (x.shape[0], 128), 1)
    first_half = (lane % HEAD_DIM) < (HEAD_DIM // 2)
    outs = []
    for j in range(x.shape[1] // 128):
        xc = x[:, j * 128:(j + 1) * 128]
        swapped = jnp.where(first_half, pltpu.roll(xc, 96, 1), pltpu.roll(xc, 32, 1))
        outs.append(xc * cos + swapped * sin_signed)
    return jnp.concatenate(outs, axis=1) if len(outs) > 1 else outs[0]


def _attend(seqs, mask, sink_ref):
    heads = [(n, h) for n in range(len(seqs)) for h in range(N_Q_HEADS)]
    col = lambda x, i: x[:, i * HEAD_DIM:(i + 1) * HEAD_DIM]

    scores = []
    for n, h in heads:
        q = col(seqs[n][0], h).astype(BF16)
        k = col(seqs[n][1], h // Q_PER_KV).astype(BF16)
        s = lax.dot_general(q, k, (((1,), (1,)), ((), ())), preferred_element_type=F32)
        scores.append(jnp.where(mask, s * (HEAD_DIM ** -0.5), NEG_INF))
    probs = []
    for (n, h), s in zip(heads, scores):
        sink = sink_ref[h]
        m = jnp.maximum(jnp.max(s, axis=-1, keepdims=True), sink)
        p = jnp.exp(s - m)
        denom = jnp.sum(p, axis=-1, keepdims=True) + jnp.exp(sink - m)
        probs.append((p / denom).astype(BF16))
    outs = [[] for _ in seqs]
    for (n, h), p in zip(heads, probs):
        outs[n].append(_dot(p, col(seqs[n][2], h // Q_PER_KV).astype(BF16)))
    return [jnp.concatenate(o, axis=1) for o in outs]


def _attn_prompt_kernel(sink_ref, q_ref, kc_ref, kp_ref, vc_ref, vp_ref, cosc_ref, sinc_ref,
                        cosp_ref, sinp_ref, g_ref, init_ref, o_ref, ks_ref, vs_ref):
    del init_ref
    c = pl.program_id(1)
    q = _rope(q_ref[...], cosc_ref[...], sinc_ref[...])
    kc = _rope(kc_ref[...], cosc_ref[...], sinc_ref[...])
    kp = _rope(kp_ref[...], cosp_ref[...], sinp_ref[...])
    kk = jnp.concatenate([kp, kc], axis=0)
    vv = jnp.concatenate([vp_ref[...], vc_ref[...]], axis=0)
    row = lax.broadcasted_iota(jnp.int32, (CHUNK, 2 * CHUNK), 0)
    col = lax.broadcasted_iota(jnp.int32, (CHUNK, 2 * CHUNK), 1)
    mask = (col > row) & (col <= row + WINDOW) & ((col >= CHUNK) | (c > 0))
    (attn,) = _attend([(q, kk, vv)], mask, sink_ref)
    o_ref[...] = (_rms_scale(attn) * g_ref[...]).astype(BF16)
    ks_ref[...] = kc
    vs_ref[...] = vc_ref[...]


def attn_prompt(z, sinks, cos_t, sin_t, mix_g_all, layer):
    blk = lambda b, c: c * BATCH + b
    prev = lambda b, c: jnp.maximum(c - 1, 0) * BATCH + b
    kcol, vcol = D_ATTN // D_KV, D_ATTN // D_KV + 1
    return pl.pallas_call(
        _attn_prompt_kernel,
        grid=(BATCH, N_CHUNKS),
        in_specs=[
            pl.BlockSpec(memory_space=pltpu.SMEM),
            pl.BlockSpec((CHUNK, D_ATTN), lambda b, c: (blk(b, c), 0)),
            pl.BlockSpec((CHUNK, D_KV), lambda b, c: (blk(b, c), kcol)),
            pl.BlockSpec((CHUNK, D_KV), lambda b, c: (prev(b, c), kcol)),
            pl.BlockSpec((CHUNK, D_KV), lambda b, c: (blk(b, c), vcol)),
            pl.BlockSpec((CHUNK, D_KV), lambda b, c: (prev(b, c), vcol)),
            pl.BlockSpec((CHUNK, 128), lambda b, c: (c, 0)),
            pl.BlockSpec((CHUNK, 128), lambda b, c: (c, 0)),
            pl.BlockSpec((CHUNK, 128), lambda b, c: (jnp.maximum(c - 1, 0), 0)),
            pl.BlockSpec((CHUNK, 128), lambda b, c: (jnp.maximum(c - 1, 0), 0)),
            pl.BlockSpec((None, 1, D_ATTN), lambda b, c: (layer, 0, 0)),
            pl.BlockSpec(memory_space=pl.ANY),
        ],
        out_specs=[
            pl.BlockSpec((CHUNK, D_ATTN), lambda b, c: (blk(b, c), 0)),
            pl.BlockSpec((None, CHUNK, D_KV), lambda b, c: (b, 0, 0)),
            pl.BlockSpec((None, CHUNK, D_KV), lambda b, c: (b, 0, 0)),
        ],
        out_shape=[
            jax.ShapeDtypeStruct((T_ALL, D_ATTN), BF16),
            jax.ShapeDtypeStruct((BATCH, CHUNK, D_KV), F32),
            jax.ShapeDtypeStruct((BATCH, CHUNK, D_KV), F32),
        ],
        input_output_aliases={11: 0},
        compiler_params=_params("arbitrary", "arbitrary"),
        name="attn_prompt",
    )(sinks, z, z, z, z, z, cos_t, sin_t, cos_t, sin_t, mix_g_all,
      jnp.zeros((T_ALL, D_ATTN), BF16))


SAMPLE_SEQS_PER_STEP = 4


def _attn_sample_kernel(sink_ref, q_ref, kn_ref, vn_ref, ck_ref, cv_ref, cos_ref, sin_ref, g_ref,
                        prev_ref, o_ref, ks_ref, vs_ref):
    del prev_ref
    row = lax.broadcasted_iota(jnp.int32, (DEC_SEQ, WINDOW + DEC_SEQ), 0)
    col = lax.broadcasted_iota(jnp.int32, (DEC_SEQ, WINDOW + DEC_SEQ), 1)
    mask = (col > row) & (col <= row + WINDOW)
    seqs = []
    for s in range(SAMPLE_SEQS_PER_STEP):
        rows = slice(s * DEC_SEQ, (s + 1) * DEC_SEQ)
        q = _rope(q_ref[rows, :], cos_ref[...], sin_ref[...])
        kn = _rope(kn_ref[rows, :], cos_ref[...], sin_ref[...])
        kk = jnp.concatenate([ck_ref[s], kn], axis=0)
        vv = jnp.concatenate([cv_ref[s], vn_ref[rows, :]], axis=0)
        ks_ref[s] = kk[DEC_SEQ:]
        vs_ref[s] = vv[DEC_SEQ:]
        seqs.append((q, kk, vv))
    normed = [_rms_scale(attn) * g_ref[...] for attn in _attend(seqs, mask, sink_ref)]
    o_ref[...] = jnp.concatenate(normed, axis=0).astype(BF16)


def attn_sample(z, attn_n, cache_k, cache_v, sinks, cos_s, sin_s, mix_g_all, layer):
    nb = SAMPLE_SEQS_PER_STEP
    rows = nb * DEC_SEQ
    base = T_PROMPT // rows
    kcol, vcol = D_ATTN // D_KV, D_ATTN // D_KV + 1
    return pl.pallas_call(
        _attn_sample_kernel,
        grid=(DEC_BATCH // nb,),
        in_specs=[
            pl.BlockSpec(memory_space=pltpu.SMEM),
            pl.BlockSpec((rows, D_ATTN), lambda i: (base + i, 0)),
            pl.BlockSpec((rows, D_KV), lambda i: (base + i, kcol)),
            pl.BlockSpec((rows, D_KV), lambda i: (base + i, vcol)),
            pl.BlockSpec((None, nb, WINDOW, D_KV), lambda i: (layer, i, 0, 0)),
            pl.BlockSpec((None, nb, WINDOW, D_KV), lambda i: (layer, i, 0, 0)),
            pl.BlockSpec((DEC_SEQ, 128), lambda i: (0, 0)),
            pl.BlockSpec((DEC_SEQ, 128), lambda i: (0, 0)),
            pl.BlockSpec((None, 1, D_ATTN), lambda i: (layer, 0, 0)),
            pl.BlockSpec(memory_space=pl.ANY),
        ],
        out_specs=[
            pl.BlockSpec((rows, D_ATTN), lambda i: (base + i, 0)),
            pl.BlockSpec((nb, WINDOW, D_KV), lambda i: (i, 0, 0)),
            pl.BlockSpec((nb, WINDOW, D_KV), lambda i: (i, 0, 0)),
        ],
        out_shape=[
            jax.ShapeDtypeStruct((T_ALL, D_ATTN), BF16),
            jax.ShapeDtypeStruct((DEC_BATCH, WINDOW, D_KV), F32),
            jax.ShapeDtypeStruct((DEC_BATCH, WINDOW, D_KV), F32),
        ],
        input_output_aliases={9: 0},
        compiler_params=_params("arbitrary"),
        name="attn_sample",
    )(sinks, z, z, z, cache_k, cache_v, cos_s, sin_s, mix_g_all, attn_n)


def _sgu_kernel(u_ref, v_ref, w_ref, bias_ref, lng_ref, lnb_ref, g_ref, o_ref, sv_ref):
    u = _gelu(u_ref[...])
    v = _gelu(v_ref[...])
    vc = v - jnp.mean(v, axis=-1, keepdims=True)
    v = vc * lax.rsqrt(jnp.mean(vc * vc, axis=-1, keepdims=True) + NORM_EPS)
    v = v * lng_ref[...] + lnb_ref[...]
    sv_ref[...] = v
    row = lax.broadcasted_iota(jnp.int32, (CHUNK, CHUNK), 0)
    col = lax.broadcasted_iota(jnp.int32, (CHUNK, CHUNK), 1)
    causal = col <= row
    vb = v.astype(BF16)
    mixed = []
    for h in range(N_SGU_HEADS):
        w = jnp.where(causal, w_ref[h], 0.0).astype(BF16)
        mixed.append(_dot(w, vb[:, h * SGU_HEAD_DIM:(h + 1) * SGU_HEAD_DIM]))
    sgu = u * (jnp.concatenate(mixed, axis=1) + bias_ref[...])
    o_ref[...] = (_rms_scale(sgu) * g_ref[...]).astype(BF16)


def sgu(z, w_mix, bias_mix, ln_g_all, ln_b_all, mix_g_all, layer):
    n_prompt = T_PROMPT // CHUNK
    kind = lambda i: jnp.where(i >= n_prompt, 1, 0)
    ucol, vcol = (D_ATTN + 2 * D_KV) // D_SGU, (D_ATTN + 2 * D_KV) // D_SGU + 1
    gcol = D_ATTN // D_SGU
    return pl.pallas_call(
        _sgu_kernel,
        grid=(T_ALL // CHUNK,),
        in_specs=[
            pl.BlockSpec((CHUNK, D_SGU), lambda i: (i, ucol)),
            pl.BlockSpec((CHUNK, D_SGU), lambda i: (i, vcol)),
            pl.BlockSpec((None, None, N_SGU_HEADS, CHUNK, CHUNK), lambda i: (layer, kind(i), 0, 0, 0)),
            pl.BlockSpec((None, None, CHUNK, D_SGU), lambda i: (layer, kind(i), 0, 0)),
            pl.BlockSpec((None, 1, D_SGU), lambda i: (layer, 0, 0)),
            pl.BlockSpec((None, 1, D_SGU), lambda i: (layer, 0, 0)),
            pl.BlockSpec((None, 1, D_SGU), lambda i: (layer, 0, gcol)),
        ],
        out_specs=[
            pl.BlockSpec((CHUNK, D_SGU), lambda i: (i, 0)),
            pl.BlockSpec((CHUNK, D_SGU), lambda i: (jnp.maximum(i - n_prompt, 0), 0)),
        ],
        out_shape=[
            jax.ShapeDtypeStruct((T_ALL, D_SGU), BF16),
            jax.ShapeDtypeStruct((T_SAMPLE, D_SGU), F32),
        ],
        compiler_params=_params("arbitrary"),
        name="sgu",
    )(z, z, w_mix, bias_mix, ln_g_all, ln_b_all, mix_g_all)


def _ssm_disc_kernel(lr_ref, li_ref, ldt_ref, br_ref, bi_ref, ar_ref, ai_ref, bbr_ref, bbi_ref):
    lr = lr_ref[...]
    li = li_ref[...]
    dt = jnp.exp(ldt_ref[...])
    mag = jnp.exp(lr * dt)
    a_re = mag * jnp.cos(li * dt)
    a_im = mag * jnp.sin(li * dt)
    nr, ni = a_re - 1.0, a_im
    den = lr * lr + li * li
    c_re = (nr * lr + ni * li) / den
    c_im = (ni * lr - nr * li) / den
    ar_ref[...] = a_re
    ai_ref[...] = a_im
    bbr_ref[...] = c_re * br_ref[...] - c_im * bi_ref[...]
    bbi_ref[...] = c_re * bi_ref[...] + c_im * br_ref[...]


def ssm_discretize(lam_re, lam_im, log_dt, b_re_t, b_im_t):
    d, g, p = lam_re.shape
    c = b_re_t.shape[2]
    return pl.pallas_call(
        _ssm_disc_kernel,
        out_shape=[
            jax.ShapeDtypeStruct((d, g, 1, p), F32),
            jax.ShapeDtypeStruct((d, g, 1, p), F32),
            jax.ShapeDtypeStruct((d, g, c, p), F32),
            jax.ShapeDtypeStruct((d, g, c, p), F32),
        ],
        compiler_params=pltpu.CompilerParams(vmem_limit_bytes=VMEM_LIMIT_BYTES),
        name="ssm_discretize",
    )(lam_re.reshape(d, g, 1, p), lam_im.reshape(d, g, 1, p), log_dt.reshape(d, g, 1, 1),
      b_re_t, b_im_t)


SCAN_COLS = 512
SCAN_BLOCKS = SCAN_COLS // 128


def _ssm_kernel(u_ref, bre_ref, bim_ref, cre_ref, cim_ref, are_ref, aim_ref, d_ref, gw_ref, gb_ref,
                g_ref, h0r_ref, h0i_ref, prev_ref, o_ref, hr_ref, hi_ref, xr_ref, xi_ref, *, ns, lc):
    del prev_ref
    half_u = D_SSM // 2
    half_s = N_STATE // 2

    @pl.when(pl.program_id(0) == 0)
    def _():
        hr_ref[...] = h0r_ref[...]
        hi_ref[...] = h0i_ref[...]

    u = u_ref[...]
    ub = u.astype(BF16)
    blocks_per_half = half_s // 128
    for hf in range(2):
        uh = ub[:, hf * half_u:(hf + 1) * half_u]
        x_re = _dot(uh, bre_ref[hf].astype(BF16))
        x_im = _dot(uh, bim_ref[hf].astype(BF16))
        for j in range(blocks_per_half):
            xr_ref[hf * blocks_per_half + j] = x_re[:, j * 128:(j + 1) * 128]
            xi_ref[hf * blocks_per_half + j] = x_im[:, j * 128:(j + 1) * 128]

    for cc in range(N_STATE // SCAN_COLS):
        blocks = range(cc * SCAN_BLOCKS, (cc + 1) * SCAN_BLOCKS)
        a_re = [jnp.broadcast_to(are_ref[:, j * 128:(j + 1) * 128], (ns, 128)) for j in blocks]
        a_im = [jnp.broadcast_to(aim_ref[:, j * 128:(j + 1) * 128], (ns, 128)) for j in blocks]

        def step(t, carry):
            rows = pl.ds(t, ns, stride=lc)
            new = []
            for n, j in enumerate(blocks):
                h_re, h_im = carry[n]
                n_re = a_re[n] * h_re - a_im[n] * h_im + xr_ref[j, rows, :]
                n_im = a_re[n] * h_im + a_im[n] * h_re + xi_ref[j, rows, :]
                xr_ref[j, rows, :] = n_re
                xi_ref[j, rows, :] = n_im
                new.append((n_re, n_im))
            return tuple(new)

        init = tuple((hr_ref[:, j * 128:(j + 1) * 128], hi_ref[:, j * 128:(j + 1) * 128]) for j in blocks)
        final = lax.fori_loop(0, lc, step, init)
        for n, j in enumerate(blocks):
            hr_ref[:, j * 128:(j + 1) * 128] = final[n][0]
            hi_ref[:, j * 128:(j + 1) * 128] = final[n][1]

    ys = []
    for hf in range(2):
        js = range(hf * blocks_per_half, (hf + 1) * blocks_per_half)
        h_re = jnp.concatenate([xr_ref[j].astype(BF16) for j in js], axis=1)
        h_im = jnp.concatenate([xi_ref[j].astype(BF16) for j in js], axis=1)
        ys.append(_dot(h_re, cre_ref[hf].astype(BF16)) - _dot(h_im, cim_ref[hf].astype(BF16)))
    y = jnp.concatenate(ys, axis=1) + d_ref[...] * u
    g = _gelu(y)
    gate = _dot(g.astype(BF16), gw_ref[...].astype(BF16)) + gb_ref[...]
    ssm = g * jax.nn.sigmoid(gate)
    o_ref[...] = (_rms_scale(ssm) * g_ref[...]).astype(BF16)


def ssm(z, prev, h0_re, h0_im, bre, bim, cre, cim, a_re, a_im, d_all, glu_w_all, glu_b_all,
        mix_g_all, layer, *, ns, lc, first_block, n_blocks):
    rows = ns * lc
    zcol = (D_ATTN + 2 * D_KV + 2 * D_SGU) // D_SSM
    gcol = (D_ATTN + D_SGU) // D_SSM
    full = lambda shape: pl.BlockSpec(shape, lambda i: (0,) * len(shape))
    lay3 = lambda shape: pl.BlockSpec((None,) + shape, lambda i: (layer,) + (0,) * len(shape))
    in_specs = [
        pl.BlockSpec((rows, D_SSM), lambda i: (first_block + i, zcol)),
        lay3((2, D_SSM // 2, N_STATE // 2)),
        lay3((2, D_SSM // 2, N_STATE // 2)),
        lay3((2, N_STATE // 2, D_SSM // 2)),
        lay3((2, N_STATE // 2, D_SSM // 2)),
        lay3((1, N_STATE)),
        lay3((1, N_STATE)),
        lay3((1, D_SSM)),
        lay3((D_SSM, D_SSM)),
        lay3((1, D_SSM)),
        pl.BlockSpec((None, 1, D_SSM), lambda i: (layer, 0, gcol)),
        full((ns, N_STATE)),
        full((ns, N_STATE)),
        pl.BlockSpec(memory_space=pl.ANY),
    ]
    args = [z, bre, bim, cre, cim, a_re, a_im, d_all, glu_w_all, glu_b_all, mix_g_all, h0_re, h0_im,
            prev]
    return pl.pallas_call(
        functools.partial(_ssm_kernel, ns=ns, lc=lc),
        grid=(n_blocks,),
        in_specs=in_specs,
        out_specs=[
            pl.BlockSpec((rows, D_SSM), lambda i: (first_block + i, 0)),
            full((ns, N_STATE)),
            full((ns, N_STATE)),
        ],
        out_shape=[
            jax.ShapeDtypeStruct((T_ALL, D_SSM), BF16),
            jax.ShapeDtypeStruct((ns, N_STATE), F32),
            jax.ShapeDtypeStruct((ns, N_STATE), F32),
        ],
        scratch_shapes=[pltpu.VMEM((N_STATE // 128, rows, 128), F32),
                        pltpu.VMEM((N_STATE // 128, rows, 128), F32)],
        input_output_aliases={13: 0},
        compiler_params=_params("arbitrary"),
        name="ssm",
    )(*args)


def _out_proj_kernel(a0_ref, a1_ref, a2_ref, w_ref, r_ref, o_ref, *, tn):
    k0 = a0_ref.shape[1]
    k1 = k0 + a1_ref.shape[1]
    for j in range(o_ref.shape[1] // tn):
        cols = slice(j * tn, (j + 1) * tn)
        acc = _dot(a0_ref[...], w_ref[:k0, cols])
        acc += _dot(a1_ref[...], w_ref[k0:k1, cols])
        acc += _dot(a2_ref[...], w_ref[k1:, cols])
        o_ref[:, cols] = r_ref[:, cols] + acc


def out_proj(a0, a1, a2, w_all, resid, layer, *, tm, tn):
    t, n = resid.shape
    k = w_all.shape[1]
    return pl.pallas_call(
        functools.partial(_out_proj_kernel, tn=tn),
        grid=(t // tm,),
        in_specs=[
            pl.BlockSpec((tm, a0.shape[1]), lambda i: (i, 0)),
            pl.BlockSpec((tm, a1.shape[1]), lambda i: (i, 0)),
            pl.BlockSpec((tm, a2.shape[1]), lambda i: (i, 0)),
            _resident((None, k, n), lambda i: (layer, 0, 0)),
            pl.BlockSpec((tm, n), lambda i: (i, 0)),
        ],
        out_specs=pl.BlockSpec((tm, n), lambda i: (i, 0)),
        out_shape=jax.ShapeDtypeStruct((t, n), F32),
        compiler_params=_params("arbitrary"),
        name="out_proj",
    )(a0, a1, a2, w_all, resid)


def _norm_kernel(x_ref, g_ref, o_ref):
    o_ref[...] = (_rms_scale(x_ref[...]) * g_ref[...]).astype(o_ref.dtype)


def norm_rows(x, g_all, layer, *, tm, dtype):
    t, k = x.shape
    return pl.pallas_call(
        _norm_kernel,
        grid=(t // tm,),
        in_specs=[pl.BlockSpec((tm, k), lambda i: (i, 0)),
                  pl.BlockSpec((None, 1, k), lambda i: (layer, 0, 0))],
        out_specs=pl.BlockSpec((tm, k), lambda i: (i, 0)),
        out_shape=jax.ShapeDtypeStruct((t, k), dtype),
        compiler_params=_params("arbitrary"),
        name="norm_rows",
    )(x, g_all)


def _router_kernel(x_ref, g_ref, w_ref, idx_ref, p_ref):
    h = (_rms_scale(x_ref[...]) * g_ref[...]).astype(BF16)
    logits = _dot(h, w_ref[...].astype(BF16))
    e = jnp.exp(logits - jnp.max(logits, axis=-1, keepdims=True))
    probs = e / jnp.sum(e, axis=-1, keepdims=True)
    lane = lax.broadcasted_iota(jnp.int32, probs.shape, 1)
    p1 = jnp.max(probs, axis=-1, keepdims=True)
    i1 = jnp.min(jnp.where(probs == p1, lane, N_EXPERTS), axis=-1, keepdims=True)
    rest = jnp.where(lane == i1, -1.0, probs)
    p2 = jnp.max(rest, axis=-1, keepdims=True)
    i2 = jnp.min(jnp.where(rest == p2, lane, N_EXPERTS), axis=-1, keepdims=True)
    tot = p1 + p2
    idx_ref[...] = jnp.where(lane == 0, i1, jnp.where(lane == 1, i2, 0))
    p_ref[...] = jnp.where(lane == 0, p1 / tot, jnp.where(lane == 1, p2 / tot, 0.0))


def router_top2(x, g_all, w_all, layer, moe_layer, *, tm):
    t, k = x.shape
    return pl.pallas_call(
        _router_kernel,
        grid=(t // tm,),
        in_specs=[pl.BlockSpec((tm, k), lambda i: (i, 0)),
                  pl.BlockSpec((None, 1, k), lambda i: (layer, 0, 0)),
                  pl.BlockSpec((None, k, N_EXPERTS), lambda i: (moe_layer, 0, 0))],
        out_specs=[pl.BlockSpec((tm, N_EXPERTS), lambda i: (i, 0)),
                   pl.BlockSpec((tm, N_EXPERTS), lambda i: (i, 0))],
        out_shape=[jax.ShapeDtypeStruct((t, N_EXPERTS), jnp.int32),
                   jax.ShapeDtypeStruct((t, N_EXPERTS), F32)],
        compiler_params=_params("arbitrary"),
        name="router_top2",
    )(x, g_all, w_all)


def _row_copy(src_hbm, row, dst, r, sem):
    return pltpu.make_async_copy(src_hbm.at[pl.ds(row, 1)], dst.at[pl.ds(r, 1)], sem)


def _gather_norm_kernel(idx_ref, act_ref, x_hbm, g_ref, o_ref, buf, sems):
    tg = buf.shape[1]
    i = pl.program_id(0)
    n = pl.num_programs(0)

    def copies(tile, fn):
        slot = tile % 2
        base = tile * tg

        def body(r, _):
            fn(_row_copy(x_hbm, idx_ref[base + r], buf.at[slot], r, sems.at[slot]))
            return 0

        lax.fori_loop(0, tg, body, 0, unroll=DMA_ISSUE_UNROLL)

    @pl.when(jnp.logical_and(i == 0, act_ref[0] > 0))
    def _():
        copies(0, lambda cp: cp.start())

    @pl.when(jnp.logical_and(i + 1 < n, act_ref[jnp.minimum(i + 1, n - 1)] > 0))
    def _():
        copies(i + 1, lambda cp: cp.start())

    @pl.when(act_ref[i] > 0)
    def _():
        copies(i, lambda cp: cp.wait())
        o_ref[...] = (_rms_scale(buf[i % 2]) * g_ref[...]).astype(BF16)

    @pl.when(act_ref[i] == 0)
    def _():
        o_ref[...] = jnp.zeros_like(o_ref)


def gather_norm(x, idx, tile_active, g_all, layer, *, tg):
    n = idx.shape[0]
    k = x.shape[1]
    return pl.pallas_call(
        _gather_norm_kernel,
        grid_spec=pltpu.PrefetchScalarGridSpec(
            num_scalar_prefetch=2,
            grid=(n // tg,),
            in_specs=[pl.BlockSpec(memory_space=pl.ANY),
                      pl.BlockSpec((None, 1, k), lambda i, idx, act: (layer, 0, 0))],
            out_specs=pl.BlockSpec((tg, k), lambda i, idx, act: (i, 0)),
            scratch_shapes=[pltpu.VMEM((2, tg, k), F32), pltpu.SemaphoreType.DMA((2,))],
        ),
        out_shape=jax.ShapeDtypeStruct((n, k), BF16),
        compiler_params=_params("arbitrary"),
        name="gather_norm",
    )(idx, tile_active, x, g_all)


def _combine_kernel(slot_ref, o_hbm, x_ref, out_ref, buf_a, buf_b, sems):
    tc = buf_a.shape[1]
    i = pl.program_id(0)
    n = pl.num_programs(0)

    def copies(tile, fn):
        slot = tile % 2
        base = tile * tc

        def body(r, _):
            fn(_row_copy(o_hbm, slot_ref[2 * (base + r)], buf_a.at[slot], r, sems.at[slot]))
            fn(_row_copy(o_hbm, slot_ref[2 * (base + r) + 1], buf_b.at[slot], r, sems.at[slot]))
            return 0

        lax.fori_loop(0, tc, body, 0, unroll=DMA_ISSUE_UNROLL)

    @pl.when(i == 0)
    def _():
        copies(0, lambda cp: cp.start())

    @pl.when(i + 1 < n)
    def _():
        copies(i + 1, lambda cp: cp.start())

    copies(i, lambda cp: cp.wait())
    out_ref[...] = x_ref[...] + (buf_a[i % 2] + buf_b[i % 2])


def moe_combine(o_sorted, slots, x, *, tc):
    t, k = x.shape
    return pl.pallas_call(
        _combine_kernel,
        grid_spec=pltpu.PrefetchScalarGridSpec(
            num_scalar_prefetch=1,
            grid=(t // tc,),
            in_specs=[pl.BlockSpec(memory_space=pl.ANY),
                      pl.BlockSpec((tc, k), lambda i, s: (i, 0))],
            out_specs=pl.BlockSpec((tc, k), lambda i, s: (i, 0)),
            scratch_shapes=[pltpu.VMEM((2, tc, k), F32), pltpu.VMEM((2, tc, k), F32),
                            pltpu.SemaphoreType.DMA((2,))],
        ),
        out_shape=jax.ShapeDtypeStruct((t, k), F32),
        compiler_params=_params("arbitrary"),
        name="moe_combine",
    )(slots, o_sorted, x)


def _ffn_kernel(te_ref, nsub_ref, x_ref, wg_ref, wu_ref, wd_ref, s_ref, *rest, nf, rb, has_resid):
    if has_resid:
        r_ref, o_ref = rest
    else:
        (o_ref,) = rest
    f = pl.program_id(1)

    @pl.when(f == 0)
    def _():
        o_ref[...] = jnp.zeros_like(o_ref)

    def row_block(r, carry):
        rows = pl.ds(pl.multiple_of(r * rb, rb), rb)
        x = x_ref[rows, :]
        gate = _dot(x, wg_ref[...].astype(BF16))
        up = _dot(x, wu_ref[...].astype(BF16))
        h = (gate * jax.nn.sigmoid(gate) * up).astype(BF16)
        o_ref[rows, :] += _dot(h, wd_ref[...].astype(BF16))
        return carry

    lax.fori_loop(0, nsub_ref[pl.program_id(0)], row_block, 0)

    @pl.when(f == nf - 1)
    def _():
        out = o_ref[...] * s_ref[...]
        if has_resid:
            out = r_ref[...] + out
        o_ref[...] = out


def grouped_ffn(x_rows, tile_expert, tile_nsub, scale, wg_all, wu_all, wd_all, lead, resid,
                *, tm, tf, rb):
    s, k = x_rows.shape
    ff = wg_all.shape[-1]
    nf = ff // tf
    n_tiles = s // tm
    nlead = len(lead)
    fblk = lambda i, f, act: jnp.where(act[i] > 0, f, nf - 1)
    w_in_spec = pl.BlockSpec((None,) * (nlead + 1) + (k, tf),
                             lambda i, f, te, act: lead + (te[i], 0, fblk(i, f, act)))
    w_out_spec = pl.BlockSpec((None,) * (nlead + 1) + (tf, k),
                              lambda i, f, te, act: lead + (te[i], fblk(i, f, act), 0))
    row_spec = lambda width: pl.BlockSpec((tm, width), lambda i, f, te, act: (i, 0))
    in_specs = [row_spec(k), w_in_spec, w_in_spec, w_out_spec, row_spec(1)]
    args = [x_rows, wg_all, wu_all, wd_all, scale]
    if resid is not None:
        in_specs.append(row_spec(k))
        args.append(resid)
    return pl.pallas_call(
        functools.partial(_ffn_kernel, nf=nf, rb=rb, has_resid=resid is not None),
        grid_spec=pltpu.PrefetchScalarGridSpec(
            num_scalar_prefetch=2,
            grid=(n_tiles, nf),
            in_specs=in_specs,
            out_specs=row_spec(k),
        ),
        out_shape=jax.ShapeDtypeStruct((s, k), F32),
        compiler_params=_params("arbitrary", "arbitrary"),
        name="grouped_ffn",
    )(tile_expert, tile_nsub, *args)


def _moe_plan(top_i, top_p, tm, rb, tg, n_tiles):
    t = top_i.shape[0]
    e = top_i.reshape(-1)
    p = top_p.reshape(-1)
    onehot = (e[:, None] == jnp.arange(N_EXPERTS, dtype=jnp.int32)[None, :]).astype(jnp.int32)
    csum = jnp.cumsum(onehot, axis=0)
    rank = jnp.take_along_axis(csum, e[:, None], axis=1)[:, 0] - 1
    counts = csum[-1]
    tiles_e = (counts + tm - 1) // tm
    tile_end = jnp.cumsum(tiles_e)
    tile_start = tile_end - tiles_e
    dest = tile_start[e] * tm + rank
    src_entry = jnp.full((n_tiles * tm,), -1, jnp.int32).at[dest].set(
        jnp.arange(2 * t, dtype=jnp.int32), unique_indices=True)
    entry = jnp.maximum(src_entry, 0)
    src_token = entry // 2
    scale = jnp.where(src_entry >= 0, p[entry], 0.0)
    tile_ids = jnp.arange(n_tiles, dtype=jnp.int32)
    tile_expert = jnp.sum((tile_ids[:, None] >= tile_end[None, :]).astype(jnp.int32), axis=1)
    active = (tile_ids < tile_end[-1]).astype(jnp.int32)
    last_expert = jnp.minimum(tile_expert[jnp.maximum(tile_end[-1] - 1, 0)], N_EXPERTS - 1)
    tile_expert = jnp.where(active > 0, jnp.minimum(tile_expert, N_EXPERTS - 1), last_expert)
    tile_valid = jnp.clip(counts[tile_expert] - (tile_ids - tile_start[tile_expert]) * tm, 0, tm) * active
    tile_nsub = (tile_valid + rb - 1) // rb
    per = tm // tg
    g_ids = jnp.arange(n_tiles * per, dtype=jnp.int32)
    g_active = (tile_valid[g_ids // per] > (g_ids % per) * tg).astype(jnp.int32)
    return (src_token, scale.reshape(-1, 1), dest.astype(jnp.int32), tile_expert.astype(jnp.int32),
            tile_nsub.astype(jnp.int32), g_active)


def _final_norm_kernel(x_ref, g_ref, op_ref, os_ref):
    i = pl.program_id(0)
    y = _rms_scale(x_ref[...]) * g_ref[...]

    @pl.when(i < T_PROMPT // CHUNK)
    def _():
        op_ref[...] = y

    @pl.when(i >= T_PROMPT // CHUNK)
    def _():
        os_ref[...] = y


def final_norm(x, g):
    n_prompt = T_PROMPT // CHUNK
    k = x.shape[1]

    def prompt_block(i):
        j = jnp.minimum(i, n_prompt - 1)
        return ((j % BATCH) * N_CHUNKS + j // BATCH, 0)

    return pl.pallas_call(
        _final_norm_kernel,
        grid=(T_ALL // CHUNK,),
        in_specs=[pl.BlockSpec((CHUNK, k), lambda i: (i, 0)),
                  pl.BlockSpec((1, k), lambda i: (0, 0))],
        out_specs=[pl.BlockSpec((CHUNK, k), prompt_block),
                   pl.BlockSpec((CHUNK, k), lambda i: (jnp.maximum(i - n_prompt, 0), 0))],
        out_shape=[jax.ShapeDtypeStruct((T_PROMPT, k), F32),
                   jax.ShapeDtypeStruct((T_SAMPLE, k), F32)],
        compiler_params=_params("arbitrary"),
        name="final_norm",
    )(x, g.reshape(1, k))


def _rope_tables(pos):
    half = HEAD_DIM // 2
    inv_freq = jnp.power(ROPE_THETA, -jnp.arange(half, dtype=F32) / half)
    ang = pos.astype(F32)[:, None] * inv_freq[None, :]
    cos, sin = jnp.cos(ang), jnp.sin(ang)
    return jnp.tile(cos, (1, 4)), jnp.tile(jnp.concatenate([-sin, sin], axis=1), (1, 2))


def _block_diag(blocks):
    *lead, n, r, c = blocks.shape
    eye = jnp.eye(n, dtype=blocks.dtype)
    return (blocks[..., :, :, None, :] * eye[:, None, :, None]).reshape(*lead, n * r, n * c)


def kernel(x_prompt, x_sample, cache_k, cache_v, state_ssm_re, state_ssm_im, norm1_g, w_in, attn_sinks, sgu_ln_g, sgu_ln_b, sgu_w, sgu_b, ssm_lambda_re, ssm_lambda_im, ssm_log_dt, ssm_b_re, ssm_b_im, ssm_c_re, ssm_c_im, ssm_d, ssm_glu_w, ssm_glu_b, mix_norm_g, w_out, norm2_g, ffn_w_gate, ffn_w_up, ffn_w_down, moe_router, moe_w_gate, moe_w_up, moe_w_down, final_norm_g):
    d = D_MODEL
    xp = x_prompt.reshape(BATCH, N_CHUNKS, CHUNK, d).transpose(1, 0, 2, 3).reshape(T_PROMPT, d)
    x = jnp.concatenate([xp, x_sample.reshape(T_SAMPLE, d)], axis=0)

    cos_p, sin_p = _rope_tables(jnp.arange(SEQ, dtype=jnp.int32))
    cos_s, sin_s = _rope_tables(PAST_LEN + jnp.arange(DEC_SEQ, dtype=jnp.int32))

    row3 = lambda a: a.reshape(DEPTH, 1, a.shape[-1])
    norm1_g3, norm2_g3, mix_g3 = row3(norm1_g), row3(norm2_g), row3(mix_norm_g)
    ln_g3, ln_b3 = row3(sgu_ln_g), row3(sgu_ln_b)
    glu_b3 = row3(ssm_glu_b)
    ssm_d3 = ssm_d.reshape(DEPTH, 1, D_SSM)
    cache_k4 = cache_k.reshape(DEPTH, DEC_BATCH, WINDOW, D_KV)
    cache_v4 = cache_v.reshape(DEPTH, DEC_BATCH, WINDOW, D_KV)
    w_in_b, w_out_b = w_in.astype(BF16), w_out.astype(BF16)

    reps = CHUNK // DEC_SEQ
    w_small = jnp.tril(sgu_w[:, :, :DEC_SEQ, :DEC_SEQ])
    eye = jnp.eye(reps, dtype=F32)
    w_sample = (eye[None, None, :, None, :, None] * w_small[:, :, None, :, None, :]).reshape(
        DEPTH, N_SGU_HEADS, CHUNK, CHUNK)
    w_mix = jnp.stack([sgu_w, w_sample], axis=1)
    b_prompt = jnp.repeat(sgu_b.transpose(0, 2, 1), SGU_HEAD_DIM, axis=2)
    b_sample = jnp.tile(b_prompt[:, :DEC_SEQ], (1, reps, 1))
    bias_mix = jnp.stack([b_prompt, b_sample], axis=1)

    a_re, a_im, bbt_re, bbt_im = ssm_discretize(
        ssm_lambda_re, ssm_lambda_im, ssm_log_dt,
        ssm_b_re.transpose(0, 1, 3, 2), ssm_b_im.transpose(0, 1, 3, 2))
    a_re3 = a_re.reshape(DEPTH, 1, N_STATE)
    a_im3 = a_im.reshape(DEPTH, 1, N_STATE)
    hg = N_SSM_GROUPS // 2
    halves = lambda blocks: _block_diag(blocks.reshape(DEPTH, 2, hg, *blocks.shape[2:]))
    bre, bim = halves(bbt_re), halves(bbt_im)
    cre = halves(ssm_c_re.transpose(0, 1, 3, 2))
    cim = halves(ssm_c_im.transpose(0, 1, 3, 2))
    zeros_state = jnp.zeros((BATCH, N_STATE), F32)

    ones_tiles = jnp.ones((T_ALL // ROW_TILE,), jnp.int32)
    ones_scale = jnp.ones((T_ALL, 1), F32)

    outs = {k: [] for k in ("kp", "vp", "hrp", "hip", "ks", "vs", "hrs", "his", "sgs")}
    for l in range(DEPTH):
        z = norm_matmul(x, norm1_g3, w_in_b, l, tm=ROW_TILE, tn=512)
        sinks = attn_sinks[l]
        attn_n, k_p, v_p = attn_prompt(z, sinks, cos_p, sin_p, mix_g3, l)
        attn_n, k_s, v_s = attn_sample(z, attn_n, cache_k4, cache_v4, sinks, cos_s, sin_s, mix_g3, l)
        sgu_n, sv_s = sgu(z, w_mix, bias_mix, ln_g3, ln_b3, mix_g3, l)
        ssm_args = (bre, bim, cre, cim, a_re3, a_im3, ssm_d3, ssm_glu_w, glu_b3, mix_g3, l)
        ssm_n, hr_p, hi_p = ssm(z, jnp.zeros((T_ALL, D_SSM), BF16), zeros_state, zeros_state, *ssm_args,
                                ns=BATCH, lc=CHUNK, first_block=0, n_blocks=N_CHUNKS)
        ssm_n, hr_s, hi_s = ssm(z, ssm_n, state_ssm_re[l].reshape(DEC_BATCH, N_STATE),
                                state_ssm_im[l].reshape(DEC_BATCH, N_STATE), *ssm_args,
                                ns=DEC_BATCH, lc=DEC_SEQ, first_block=T_PROMPT // T_SAMPLE, n_blocks=1)
        x = out_proj(attn_n, sgu_n, ssm_n, w_out_b, x, l, tm=ROW_TILE, tn=512)

        outs["kp"].append(k_p.reshape(BATCH, WINDOW, N_KV_HEADS, HEAD_DIM))
        outs["vp"].append(v_p.reshape(BATCH, WINDOW, N_KV_HEADS, HEAD_DIM))
        outs["hrp"].append(hr_p.reshape(BATCH, N_SSM_GROUPS, SSM_STATE))
        outs["hip"].append(hi_p.reshape(BATCH, N_SSM_GROUPS, SSM_STATE))
        outs["ks"].append(k_s.reshape(DEC_BATCH, WINDOW, N_KV_HEADS, HEAD_DIM))
        outs["vs"].append(v_s.reshape(DEC_BATCH, WINDOW, N_KV_HEADS, HEAD_DIM))
        outs["hrs"].append(hr_s.reshape(DEC_BATCH, N_SSM_GROUPS, SSM_STATE))
        outs["his"].append(hi_s.reshape(DEC_BATCH, N_SSM_GROUPS, SSM_STATE))
        outs["sgs"].append(sv_s.reshape(DEC_BATCH, DEC_SEQ, D_SGU))

        i = l // 2
        if l % 2 == 0:
            h2 = norm_rows(x, norm2_g3, l, tm=ROW_TILE, dtype=BF16)
            x = grouped_ffn(h2, jnp.full_like(ones_tiles, i), ones_tiles * (ROW_TILE // DENSE_ROW_BLOCK),
                            ones_scale, ffn_w_gate, ffn_w_up, ffn_w_down, (), x,
                            tm=ROW_TILE, tf=FF_TILE, rb=DENSE_ROW_BLOCK)
        else:
            top_i, top_p = router_top2(x, norm2_g3, moe_router, l, i, tm=ROW_TILE)
            src_token, scale, slots, tile_expert, tile_nsub, gather_active = _moe_plan(
                top_i[:, :2], top_p[:, :2], MOE_ROW_TILE, MOE_ROW_BLOCK, GATHER_TILE, MOE_TILES)
            xs = gather_norm(x, src_token, gather_active, norm2_g3, l, tg=GATHER_TILE)
            o_sorted = grouped_ffn(xs, tile_expert, tile_nsub, scale,
                                   moe_w_gate, moe_w_up, moe_w_down, (i,), None,
                                   tm=MOE_ROW_TILE, tf=FF_TILE, rb=MOE_ROW_BLOCK)
            x = moe_combine(o_sorted, slots, x, tc=GATHER_TILE)

    y_prompt, y_sample = final_norm(x, final_norm_g)
    st = lambda key: jnp.stack(outs[key])
    return (y_prompt.reshape(BATCH, SEQ, d), y_sample.reshape(DEC_BATCH, DEC_SEQ, d),
            st("kp"), st("vp"), st("hrp"), st("hip"), st("ks"), st("vs"), st("hrs"), st("his"), st("sgs"))
```

```python
import functools
import math

import jax
import jax.numpy as jnp
from jax import lax
from jax.experimental import pallas as pl
from jax.experimental.pallas import tpu as pltpu

F32 = jnp.float32
BF16 = jnp.bfloat16

D_MODEL = 2048
BATCH = 4
SEQ = 2048
DEPTH = 4
DEC_BATCH = 32
DEC_SEQ = 8
PAST_LEN = 16384
HEAD_DIM = 64
D_ATTN = 1024
N_Q_HEADS = 16
N_KV_HEADS = 4
Q_PER_KV = 4
D_KV = 256
WINDOW = 128
ROPE_THETA = 10000.0
D_SGU = 512
N_SGU_HEADS = 4
SGU_HEAD_DIM = 128
D_SSM = 512
SSM_GROUP = 16
N_SSM_GROUPS = 32
SSM_STATE = 64
N_STATE = N_SSM_GROUPS * SSM_STATE
D_IN_PROJ = 3072
N_EXPERTS = 8
NORM_EPS = 1e-6
NEG_INF = -1e30
SQRT_HALF = math.sqrt(0.5)

CHUNK = 128
N_CHUNKS = SEQ // CHUNK
T_PROMPT = BATCH * SEQ
T_SAMPLE = DEC_BATCH * DEC_SEQ
T_ALL = T_PROMPT + T_SAMPLE

VMEM_LIMIT_BYTES = 56 * 1024 * 1024

ROW_TILE = 768
DENSE_ROW_BLOCK = ROW_TILE
MOE_ROW_TILE = 1152
MOE_ROW_BLOCK = MOE_ROW_TILE // 2
MOE_TILES = (2 * T_ALL) // MOE_ROW_TILE + N_EXPERTS
FF_TILE = 256
GATHER_TILE = 384
DMA_ISSUE_UNROLL = 8


def _params(*sem):
    return pltpu.CompilerParams(dimension_semantics=sem, vmem_limit_bytes=VMEM_LIMIT_BYTES)


def _rms_scale(x):
    return x * lax.rsqrt(jnp.mean(x * x, axis=-1, keepdims=True) + NORM_EPS)


def _gelu(x):
    return 0.5 * x * (1.0 + lax.erf(x * SQRT_HALF))


def _dot(a, b):
    return jnp.dot(a, b, preferred_element_type=F32)


def _norm_matmul_kernel(x_ref, g_ref, w_ref, o_ref, *, tn):
    h = (_rms_scale(x_ref[...]) * g_ref[...]).astype(BF16)
    for j in range(o_ref.shape[1] // tn):
        o_ref[:, j * tn:(j + 1) * tn] = _dot(h, w_ref[:, j * tn:(j + 1) * tn])


def _resident(block_shape, index_map):
    return pl.BlockSpec(block_shape, index_map, pipeline_mode=pl.Buffered(1))


def norm_matmul(x, g_all, w_all, layer, *, tm, tn):
    t, k = x.shape
    n = w_all.shape[-1]
    return pl.pallas_call(
        functools.partial(_norm_matmul_kernel, tn=tn),
        grid=(t // tm,),
        in_specs=[
            pl.BlockSpec((tm, k), lambda i: (i, 0)),
            pl.BlockSpec((None, 1, k), lambda i: (layer, 0, 0)),
            _resident((None, k, n), lambda i: (layer, 0, 0)),
        ],
        out_specs=pl.BlockSpec((tm, n), lambda i: (i, 0)),
        out_shape=jax.ShapeDtypeStruct((t, n), F32),
        compiler_params=_params("arbitrary"),
        name="norm_matmul",
    )(x, g_all, w_all)


def _rope(x, cos, sin_signed):
    lane = lax.broadcasted_iota(jnp.int32, (x.shape[0], 128), 1)
    first_half = (lane % HEAD_DIM) < (HEAD_DIM // 2)
    outs = []
    for j in range(x.shape[1] // 128):
        xc = x[:, j * 128:(j + 1) * 128]
        swapped = jnp.where(first_half, pltpu.roll(xc, 96, 1), pltpu.roll(xc, 32, 1))
        outs.append(xc * cos + swapped * sin_signed)
    return jnp.concatenate(outs, axis=1) if len(outs) > 1 else outs[0]


def _attend(seqs, mask, sink_ref):
    heads = [(n, h) for n in range(len(seqs)) for h in range(N_Q_HEADS)]
    col = lambda x, i: x[:, i * HEAD_DIM:(i + 1) * HEAD_DIM]

    scores = []
    for n, h in heads:
        q = col(seqs[n][0], h).astype(BF16)
        k = col(seqs[n][1], h // Q_PER_KV).astype(BF16)
        s = lax.dot_general(q, k, (((1,), (1,)), ((), ())), preferred_element_type=F32)
        scores.append(jnp.where(mask, s * (HEAD_DIM ** -0.5), NEG_INF))
    probs = []
    for (n, h), s in zip(heads, scores):
        sink = sink_ref[h]
        m = jnp.maximum(jnp.max(s, axis=-1, keepdims=True), sink)
        p = jnp.exp(s - m)
        denom = jnp.sum(p, axis=-1, keepdims=True) + jnp.exp(sink - m)
        probs.append((p / denom).astype(BF16))
    outs = [[] for _ in seqs]
    for (n, h), p in zip(heads, probs):
        outs[n].append(_dot(p, col(seqs[n][2], h // Q_PER_KV).astype(BF16)))
    return [jnp.concatenate(o, axis=1) for o in outs]


def _attn_prompt_kernel(sink_ref, q_ref, kc_ref, kp_ref, vc_ref, vp_ref, cosc_ref, sinc_ref,
                        cosp_ref, sinp_ref, g_ref, init_ref, o_ref, ks_ref, vs_ref):
    del init_ref
    c = pl.program_id(1)
    q = _rope(q_ref[...], cosc_ref[...], sinc_ref[...])
    kc = _rope(kc_ref[...], cosc_ref[...], sinc_ref[...])
    kp = _rope(kp_ref[...], cosp_ref[...], sinp_ref[...])
    kk = jnp.concatenate([kp, kc], axis=0)
    vv = jnp.concatenate([vp_ref[...], vc_ref[...]], axis=0)
    row = lax.broadcasted_iota(jnp.int32, (CHUNK, 2 * CHUNK), 0)
    col = lax.broadcasted_iota(jnp.int32, (CHUNK, 2 * CHUNK), 1)
    mask = (col > row) & (col <= row + WINDOW) & ((col >= CHUNK) | (c > 0))
    (attn,) = _attend([(q, kk, vv)], mask, sink_ref)
    o_ref[...] = (_rms_scale(attn) * g_ref[...]).astype(BF16)
    ks_ref[...] = kc
    vs_ref[...] = vc_ref[...]


def attn_prompt(z, sinks, cos_t, sin_t, mix_g_all, layer):
    blk = lambda b, c: c * BATCH + b
    prev = lambda b, c: jnp.maximum(c - 1, 0) * BATCH + b
    kcol, vcol = D_ATTN // D_KV, D_ATTN // D_KV + 1
    return pl.pallas_call(
        _attn_prompt_kernel,
        grid=(BATCH, N_CHUNKS),
        in_specs=[
            pl.BlockSpec(memory_space=pltpu.SMEM),
            pl.BlockSpec((CHUNK, D_ATTN), lambda b, c: (blk(b, c), 0)),
            pl.BlockSpec((CHUNK, D_KV), lambda b, c: (blk(b, c), kcol)),
            pl.BlockSpec((CHUNK, D_KV), lambda b, c: (prev(b, c), kcol)),
            pl.BlockSpec((CHUNK, D_KV), lambda b, c: (blk(b, c), vcol)),
            pl.BlockSpec((CHUNK, D_KV), lambda b, c: (prev(b, c), vcol)),
            pl.BlockSpec((CHUNK, 128), lambda b, c: (c, 0)),
            pl.BlockSpec((CHUNK, 128), lambda b, c: (c, 0)),
            pl.BlockSpec((CHUNK, 128), lambda b, c: (jnp.maximum(c - 1, 0), 0)),
            pl.BlockSpec((CHUNK, 128), lambda b, c: (jnp.maximum(c - 1, 0), 0)),
            pl.BlockSpec((None, 1, D_ATTN), lambda b, c: (layer, 0, 0)),
            pl.BlockSpec(memory_space=pl.ANY),
        ],
        out_specs=[
            pl.BlockSpec((CHUNK, D_ATTN), lambda b, c: (blk(b, c), 0)),
            pl.BlockSpec((None, CHUNK, D_KV), lambda b, c: (b, 0, 0)),
            pl.BlockSpec((None, CHUNK, D_KV), lambda b, c: (b, 0, 0)),
        ],
        out_shape=[
            jax.ShapeDtypeStruct((T_ALL, D_ATTN), BF16),
            jax.ShapeDtypeStruct((BATCH, CHUNK, D_KV), F32),
            jax.ShapeDtypeStruct((BATCH, CHUNK, D_KV), F32),
        ],
        input_output_aliases={11: 0},
        compiler_params=_params("arbitrary", "arbitrary"),
        name="attn_prompt",
    )(sinks, z, z, z, z, z, cos_t, sin_t, cos_t, sin_t, mix_g_all,
      jnp.zeros((T_ALL, D_ATTN), BF16))


SAMPLE_SEQS_PER_STEP = 4


def _attn_sample_kernel(sink_ref, q_ref, kn_ref, vn_ref, ck_ref, cv_ref, cos_ref, sin_ref, g_ref,
                        prev_ref, o_ref, ks_ref, vs_ref):
    del prev_ref
    row = lax.broadcasted_iota(jnp.int32, (DEC_SEQ, WINDOW + DEC_SEQ), 0)
    col = lax.broadcasted_iota(jnp.int32, (DEC_SEQ, WINDOW + DEC_SEQ), 1)
    mask = (col > row) & (col <= row + WINDOW)
    seqs = []
    for s in range(SAMPLE_SEQS_PER_STEP):
        rows = slice(s * DEC_SEQ, (s + 1) * DEC_SEQ)
        q = _rope(q_ref[rows, :], cos_ref[...], sin_ref[...])
        kn = _rope(kn_ref[rows, :], cos_ref[...], sin_ref[...])
        kk = jnp.concatenate([ck_ref[s], kn], axis=0)
        vv = jnp.concatenate([cv_ref[s], vn_ref[rows, :]], axis=0)
        ks_ref[s] = kk[DEC_SEQ:]
        vs_ref[s] = vv[DEC_SEQ:]
        seqs.append((q, kk, vv))
    normed = [_rms_scale(attn) * g_ref[...] for attn in _attend(seqs, mask, sink_ref)]
    o_ref[...] = jnp.concatenate(normed, axis=0).astype(BF16)


def attn_sample(z, attn_n, cache_k, cache_v, sinks, cos_s, sin_s, mix_g_all, layer):
    nb = SAMPLE_SEQS_PER_STEP
    rows = nb * DEC_SEQ
    base = T_PROMPT // rows
    kcol, vcol = D_ATTN // D_KV, D_ATTN // D_KV + 1
    return pl.pallas_call(
        _attn_sample_kernel,
        grid=(DEC_BATCH // nb,),
        in_specs=[
            pl.BlockSpec(memory_space=pltpu.SMEM),
            pl.BlockSpec((rows, D_ATTN), lambda i: (base + i, 0)),
            pl.BlockSpec((rows, D_KV), lambda i: (base + i, kcol)),
            pl.BlockSpec((rows, D_KV), lambda i: (base + i, vcol)),
            pl.BlockSpec((None, nb, WINDOW, D_KV), lambda i: (layer, i, 0, 0)),
            pl.BlockSpec((None, nb, WINDOW, D_KV), lambda i: (layer, i, 0, 0)),
            pl.BlockSpec((DEC_SEQ, 128), lambda i: (0, 0)),
            pl.BlockSpec((DEC_SEQ, 128), lambda i: (0, 0)),
            pl.BlockSpec((None, 1, D_ATTN), lambda i: (layer, 0, 0)),
            pl.BlockSpec(memory_space=pl.ANY),
        ],
        out_specs=[
            pl.BlockSpec((rows, D_ATTN), lambda i: (base + i, 0)),
            pl.BlockSpec((nb, WINDOW, D_KV), lambda i: (i, 0, 0)),
            pl.BlockSpec((nb, WINDOW, D_KV), lambda i: (i, 0, 0)),
        ],
        out_shape=[
            jax.ShapeDtypeStruct((T_ALL, D_ATTN), BF16),
            jax.ShapeDtypeStruct((DEC_BATCH, WINDOW, D_KV), F32),
            jax.ShapeDtypeStruct((DEC_BATCH, WINDOW, D_KV), F32),
        ],
        input_output_aliases={9: 0},
        compiler_params=_params("arbitrary"),
        name="attn_sample",
    )(sinks, z, z, z, cache_k, cache_v, cos_s, sin_s, mix_g_all, attn_n)


def _sgu_kernel(u_ref, v_ref, w_ref, bias_ref, lng_ref, lnb_ref, g_ref, o_ref, sv_ref):
    u = _gelu(u_ref[...])
    v = _gelu(v_ref[...])
    vc = v - jnp.mean(v, axis=-1, keepdims=True)
    v = vc * lax.rsqrt(jnp.mean(vc * vc, axis=-1, keepdims=True) + NORM_EPS)
    v = v * lng_ref[...] + lnb_ref[...]
    sv_ref[...] = v
    row = lax.broadcasted_iota(jnp.int32, (CHUNK, CHUNK), 0)
    col = lax.broadcasted_iota(jnp.int32, (CHUNK, CHUNK), 1)
    causal = col <= row
    vb = v.astype(BF16)
    mixed = []
    for h in range(N_SGU_HEADS):
        w = jnp.where(causal, w_ref[h], 0.0).astype(BF16)
        mixed.append(_dot(w, vb[:, h * SGU_HEAD_DIM:(h + 1) * SGU_HEAD_DIM]))
    sgu = u * (jnp.concatenate(mixed, axis=1) + bias_ref[...])
    o_ref[...] = (_rms_scale(sgu) * g_ref[...]).astype(BF16)


def sgu(z, w_mix, bias_mix, ln_g_all, ln_b_all, mix_g_all, layer):
    n_prompt = T_PROMPT // CHUNK
    kind = lambda i: jnp.where(i >= n_prompt, 1, 0)
    ucol, vcol = (D_ATTN + 2 * D_KV) // D_SGU, (D_ATTN + 2 * D_KV) // D_SGU + 1
    gcol = D_ATTN // D_SGU
    return pl.pallas_call(
        _sgu_kernel,
        grid=(T_ALL // CHUNK,),
        in_specs=[
            pl.BlockSpec((CHUNK, D_SGU), lambda i: (i, ucol)),
            pl.BlockSpec((CHUNK, D_SGU), lambda i: (i, vcol)),
            pl.BlockSpec((None, None, N_SGU_HEADS, CHUNK, CHUNK), lambda i: (layer, kind(i), 0, 0, 0)),
            pl.BlockSpec((None, None, CHUNK, D_SGU), lambda i: (layer, kind(i), 0, 0)),
            pl.BlockSpec((None, 1, D_SGU), lambda i: (layer, 0, 0)),
            pl.BlockSpec((None, 1, D_SGU), lambda i: (layer, 0, 0)),
            pl.BlockSpec((None, 1, D_SGU), lambda i: (layer, 0, gcol)),
        ],
        out_specs=[
            pl.BlockSpec((CHUNK, D_SGU), lambda i: (i, 0)),
            pl.BlockSpec((CHUNK, D_SGU), lambda i: (jnp.maximum(i - n_prompt, 0), 0)),
        ],
        out_shape=[
            jax.ShapeDtypeStruct((T_ALL, D_SGU), BF16),
            jax.ShapeDtypeStruct((T_SAMPLE, D_SGU), F32),
        ],
        compiler_params=_params("arbitrary"),
        name="sgu",
    )(z, z, w_mix, bias_mix, ln_g_all, ln_b_all, mix_g_all)


def _ssm_disc_kernel(lr_ref, li_ref, ldt_ref, br_ref, bi_ref, ar_ref, ai_ref, bbr_ref, bbi_ref):
    lr = lr_ref[...]
    li = li_ref[...]
    dt = jnp.exp(ldt_ref[...])
    mag = jnp.exp(lr * dt)
    a_re = mag * jnp.cos(li * dt)
    a_im = mag * jnp.sin(li * dt)
    nr, ni = a_re - 1.0, a_im
    den = lr * lr + li * li
    c_re = (nr * lr + ni * li) / den
    c_im = (ni * lr - nr * li) / den
    ar_ref[...] = a_re
    ai_ref[...] = a_im
    bbr_ref[...] = c_re * br_ref[...] - c_im * bi_ref[...]
    bbi_ref[...] = c_re * bi_ref[...] + c_im * br_ref[...]


def ssm_discretize(lam_re, lam_im, log_dt, b_re_t, b_im_t):
    d, g, p = lam_re.shape
    c = b_re_t.shape[2]
    return pl.pallas_call(
        _ssm_disc_kernel,
        out_shape=[
            jax.ShapeDtypeStruct((d, g, 1, p), F32),
            jax.ShapeDtypeStruct((d, g, 1, p), F32),
            jax.ShapeDtypeStruct((d, g, c, p), F32),
            jax.ShapeDtypeStruct((d, g, c, p), F32),
        ],
        compiler_params=pltpu.CompilerParams(vmem_limit_bytes=VMEM_LIMIT_BYTES),
        name="ssm_discretize",
    )(lam_re.reshape(d, g, 1, p), lam_im.reshape(d, g, 1, p), log_dt.reshape(d, g, 1, 1),
      b_re_t, b_im_t)


SCAN_COLS = 512
SCAN_BLOCKS = SCAN_COLS // 128


def _ssm_kernel(u_ref, bre_ref, bim_ref, cre_ref, cim_ref, are_ref, aim_ref, d_ref, gw_ref, gb_ref,
                g_ref, h0r_ref, h0i_ref, prev_ref, o_ref, hr_ref, hi_ref, xr_ref, xi_ref, *, ns, lc):
    del prev_ref
    half_u = D_SSM // 2
    half_s = N_STATE // 2

    @pl.when(pl.program_id(0) == 0)
    def _():
        hr_ref[...] = h0r_ref[...]
        hi_ref[...] = h0i_ref[...]

    u = u_ref[...]
    ub = u.astype(BF16)
    blocks_per_half = half_s // 128
    for hf in range(2):
        uh = ub[:, hf * half_u:(hf + 1) * half_u]
        x_re = _dot(uh, bre_ref[hf].astype(BF16))
        x_im = _dot(uh, bim_ref[hf].astype(BF16))
        for j in range(blocks_per_half):
            xr_ref[hf * blocks_per_half + j] = x_re[:, j * 128:(j + 1) * 128]
            xi_ref[hf * blocks_per_half + j] = x_im[:, j * 128:(j + 1) * 128]

    for cc in range(N_STATE // SCAN_COLS):
        blocks = range(cc * SCAN_BLOCKS, (cc + 1) * SCAN_BLOCKS)
        a_re = [jnp.broadcast_to(are_ref[:, j * 128:(j + 1) * 128], (ns, 128)) for j in blocks]
        a_im = [jnp.broadcast_to(aim_ref[:, j * 128:(j + 1) * 128], (ns, 128)) for j in blocks]

        def step(t, carry):
            rows = pl.ds(t, ns, stride=lc)
            new = []
            for n, j in enumerate(blocks):
                h_re, h_im = carry[n]
                n_re = a_re[n] * h_re - a_im[n] * h_im + xr_ref[j, rows, :]
                n_im = a_re[n] * h_im + a_im[n] * h_re + xi_ref[j, rows, :]
                xr_ref[j, rows, :] = n_re
                xi_ref[j, rows, :] = n_im
                new.append((n_re, n_im))
            return tuple(new)

        init = tuple((hr_ref[:, j * 128:(j + 1) * 128], hi_ref[:, j * 128:(j + 1) * 128]) for j in blocks)
        final = lax.fori_loop(0, lc, step, init)
        for n, j in enumerate(blocks):
            hr_ref[:, j * 128:(j + 1) * 128] = final[n][0]
            hi_ref[:, j * 128:(j + 1) * 128] = final[n][1]

    ys = []
    for hf in range(2):
        js = range(hf * blocks_per_half, (hf + 1) * blocks_per_half)
        h_re = jnp.concatenate([xr_ref[j].astype(BF16) for j in js], axis=1)
        h_im = jnp.concatenate([xi_ref[j].astype(BF16) for j in js], axis=1)
        ys.append(_dot(h_re, cre_ref[hf].astype(BF16)) - _dot(h_im, cim_ref[hf].astype(BF16)))
    y = jnp.concatenate(ys, axis=1) + d_ref[...] * u
    g = _gelu(y)
    gate = _dot(g.astype(BF16), gw_ref[...].astype(BF16)) + gb_ref[...]
    ssm = g * jax.nn.sigmoid(gate)
    o_ref[...] = (_rms_scale(ssm) * g_ref[...]).astype(BF16)


def ssm(z, prev, h0_re, h0_im, bre, bim, cre, cim, a_re, a_im, d_all, glu_w_all, glu_b_all,
        mix_g_all, layer, *, ns, lc, first_block, n_blocks):
    rows = ns * lc
    zcol = (D_ATTN + 2 * D_KV + 2 * D_SGU) // D_SSM
    gcol = (D_ATTN + D_SGU) // D_SSM
    full = lambda shape: pl.BlockSpec(shape, lambda i: (0,) * len(shape))
    lay3 = lambda shape: pl.BlockSpec((None,) + shape, lambda i: (layer,) + (0,) * len(shape))
    in_specs = [
        pl.BlockSpec((rows, D_SSM), lambda i: (first_block + i, zcol)),
        lay3((2, D_SSM // 2, N_STATE // 2)),
        lay3((2, D_SSM // 2, N_STATE // 2)),
        lay3((2, N_STATE // 2, D_SSM // 2)),
        lay3((2, N_STATE // 2, D_SSM // 2)),
        lay3((1, N_STATE)),
        lay3((1, N_STATE)),
        lay3((1, D_SSM)),
        lay3((D_SSM, D_SSM)),
        lay3((1, D_SSM)),
        pl.BlockSpec((None, 1, D_SSM), lambda i: (layer, 0, gcol)),
        full((ns, N_STATE)),
        full((ns, N_STATE)),
        pl.BlockSpec(memory_space=pl.ANY),
    ]
    args = [z, bre, bim, cre, cim, a_re, a_im, d_all, glu_w_all, glu_b_all, mix_g_all, h0_re, h0_im,
            prev]
    return pl.pallas_call(
        functools.partial(_ssm_kernel, ns=ns, lc=lc),
        grid=(n_blocks,),
        in_specs=in_specs,
        out_specs=[
            pl.BlockSpec((rows, D_SSM), lambda i: (first_block + i, 0)),
            full((ns, N_STATE)),
            full((ns, N_STATE)),
        ],
        out_shape=[
            jax.ShapeDtypeStruct((T_ALL, D_SSM), BF16),
            jax.ShapeDtypeStruct((ns, N_STATE), F32),
            jax.ShapeDtypeStruct((ns, N_STATE), F32),
        ],
        scratch_shapes=[pltpu.VMEM((N_STATE // 128, rows, 128), F32),
                        pltpu.VMEM((N_STATE // 128, rows, 128), F32)],
        input_output_aliases={13: 0},
        compiler_params=_params("arbitrary"),
        name="ssm",
    )(*args)


def _out_proj_kernel(a0_ref, a1_ref, a2_ref, w_ref, r_ref, o_ref, *, tn):
    k0 = a0_ref.shape[1]
    k1 = k0 + a1_ref.shape[1]
    for j in range(o_ref.shape[1] // tn):
        cols = slice(j * tn, (j + 1) * tn)
        acc = _dot(a0_ref[...], w_ref[:k0, cols])
        acc += _dot(a1_ref[...], w_ref[k0:k1, cols])
        acc += _dot(a2_ref[...], w_ref[k1:, cols])
        o_ref[:, cols] = r_ref[:, cols] + acc


def out_proj(a0, a1, a2, w_all, resid, layer, *, tm, tn):
    t, n = resid.shape
    k = w_all.shape[1]
    return pl.pallas_call(
        functools.partial(_out_proj_kernel, tn=tn),
        grid=(t // tm,),
        in_specs=[
            pl.BlockSpec((tm, a0.shape[1]), lambda i: (i, 0)),
            pl.BlockSpec((tm, a1.shape[1]), lambda i: (i, 0)),
            pl.BlockSpec((tm, a2.shape[1]), lambda i: (i, 0)),
            _resident((None, k, n), lambda i: (layer, 0, 0)),
            pl.BlockSpec((tm, n), lambda i: (i, 0)),
        ],
        out_specs=pl.BlockSpec((tm, n), lambda i: (i, 0)),
        out_shape=jax.ShapeDtypeStruct((t, n), F32),
        compiler_params=_params("arbitrary"),
        name="out_proj",
    )(a0, a1, a2, w_all, resid)


def _norm_kernel(x_ref, g_ref, o_ref):
    o_ref[...] = (_rms_scale(x_ref[...]) * g_ref[...]).astype(o_ref.dtype)


def norm_rows(x, g_all, layer, *, tm, dtype):
    t, k = x.shape
    return pl.pallas_call(
        _norm_kernel,
        grid=(t // tm,),
        in_specs=[pl.BlockSpec((tm, k), lambda i: (i, 0)),
                  pl.BlockSpec((None, 1, k), lambda i: (layer, 0, 0))],
        out_specs=pl.BlockSpec((tm, k), lambda i: (i, 0)),
        out_shape=jax.ShapeDtypeStruct((t, k), dtype),
        compiler_params=_params("arbitrary"),
        name="norm_rows",
    )(x, g_all)


def _router_kernel(x_ref, g_ref, w_ref, idx_ref, p_ref):
    h = (_rms_scale(x_ref[...]) * g_ref[...]).astype(BF16)
    logits = _dot(h, w_ref[...].astype(BF16))
    e = jnp.exp(logits - jnp.max(logits, axis=-1, keepdims=True))
    probs = e / jnp.sum(e, axis=-1, keepdims=True)
    lane = lax.broadcasted_iota(jnp.int32, probs.shape, 1)
    p1 = jnp.max(probs, axis=-1, keepdims=True)
    i1 = jnp.min(jnp.where(probs == p1, lane, N_EXPERTS), axis=-1, keepdims=True)
    rest = jnp.where(lane == i1, -1.0, probs)
    p2 = jnp.max(rest, axis=-1, keepdims=True)
    i2 = jnp.min(jnp.where(rest == p2, lane, N_EXPERTS), axis=-1, keepdims=True)
    tot = p1 + p2
    idx_ref[...] = jnp.where(lane == 0, i1, jnp.where(lane == 1, i2, 0))
    p_ref[...] = jnp.where(lane == 0, p1 / tot, jnp.where(lane == 1, p2 / tot, 0.0))


def router_top2(x, g_all, w_all, layer, moe_layer, *, tm):
    t, k = x.shape
    return pl.pallas_call(
        _router_kernel,
        grid=(t // tm,),
        in_specs=[pl.BlockSpec((tm, k), lambda i: (i, 0)),
                  pl.BlockSpec((None, 1, k), lambda i: (layer, 0, 0)),
                  pl.BlockSpec((None, k, N_EXPERTS), lambda i: (moe_layer, 0, 0))],
        out_specs=[pl.BlockSpec((tm, N_EXPERTS), lambda i: (i, 0)),
                   pl.BlockSpec((tm, N_EXPERTS), lambda i: (i, 0))],
        out_shape=[jax.ShapeDtypeStruct((t, N_EXPERTS), jnp.int32),
                   jax.ShapeDtypeStruct((t, N_EXPERTS), F32)],
        compiler_params=_params("arbitrary"),
        name="router_top2",
    )(x, g_all, w_all)


def _row_copy(src_hbm, row, dst, r, sem):
    return pltpu.make_async_copy(src_hbm.at[pl.ds(row, 1)], dst.at[pl.ds(r, 1)], sem)


def _start_on_thread(copy, thread):
    copy.start(priority=thread)


def _wait_copy(copy, thread):
    del thread
    copy.wait()


def _gather_norm_kernel(idx_ref, act_ref, x_hbm, g_ref, o_ref, buf, sems):
    tg = buf.shape[1]
    i = pl.program_id(0)
    n = pl.num_programs(0)

    def copies(tile, fn):
        slot = tile % 2
        base = tile * tg

        def body(r, _):
            for thread in range(2):
                row = 2 * r + thread
                fn(_row_copy(x_hbm, idx_ref[base + row], buf.at[slot], row, sems.at[slot]), thread)
            return 0

        lax.fori_loop(0, tg // 2, body, 0, unroll=DMA_ISSUE_UNROLL // 2)

    @pl.when(jnp.logical_and(i == 0, act_ref[0] > 0))
    def _():
        copies(0, _start_on_thread)

    @pl.when(jnp.logical_and(i + 1 < n, act_ref[jnp.minimum(i + 1, n - 1)] > 0))
    def _():
        copies(i + 1, _start_on_thread)

    @pl.when(act_ref[i] > 0)
    def _():
        copies(i, _wait_copy)
        o_ref[...] = (_rms_scale(buf[i % 2]) * g_ref[...]).astype(BF16)

    @pl.when(act_ref[i] == 0)
    def _():
        o_ref[...] = jnp.zeros_like(o_ref)


def gather_norm(x, idx, tile_active, g_all, layer, *, tg):
    n = idx.shape[0]
    k = x.shape[1]
    return pl.pallas_call(
        _gather_norm_kernel,
        grid_spec=pltpu.PrefetchScalarGridSpec(
            num_scalar_prefetch=2,
            grid=(n // tg,),
            in_specs=[pl.BlockSpec(memory_space=pl.ANY),
                      pl.BlockSpec((None, 1, k), lambda i, idx, act: (layer, 0, 0))],
            out_specs=pl.BlockSpec((tg, k), lambda i, idx, act: (i, 0)),
            scratch_shapes=[pltpu.VMEM((2, tg, k), F32), pltpu.SemaphoreType.DMA((2,))],
        ),
        out_shape=jax.ShapeDtypeStruct((n, k), BF16),
        compiler_params=_params("arbitrary"),
        name="gather_norm",
    )(idx, tile_active, x, g_all)


def _combine_kernel(slot_ref, o_hbm, x_ref, out_ref, buf_a, buf_b, sems):
    tc = buf_a.shape[1]
    i = pl.program_id(0)
    n = pl.num_programs(0)

    def copies(tile, fn):
        slot = tile % 2
        base = tile * tc

        def body(r, _):
            fn(_row_copy(o_hbm, slot_ref[2 * (base + r)], buf_a.at[slot], r, sems.at[slot]), 0)
            fn(_row_copy(o_hbm, slot_ref[2 * (base + r) + 1], buf_b.at[slot], r, sems.at[slot]), 1)
            return 0

        lax.fori_loop(0, tc, body, 0, unroll=DMA_ISSUE_UNROLL)

    @pl.when(i == 0)
    def _():
        copies(0, _start_on_thread)

    @pl.when(i + 1 < n)
    def _():
        copies(i + 1, _start_on_thread)

    copies(i, _wait_copy)
    out_ref[...] = x_ref[...] + (buf_a[i % 2] + buf_b[i % 2])


def moe_combine(o_sorted, slots, x, *, tc):
    t, k = x.shape
    return pl.pallas_call(
        _combine_kernel,
        grid_spec=pltpu.PrefetchScalarGridSpec(
            num_scalar_prefetch=1,
            grid=(t // tc,),
            in_specs=[pl.BlockSpec(memory_space=pl.ANY),
                      pl.BlockSpec((tc, k), lambda i, s: (i, 0))],
            out_specs=pl.BlockSpec((tc, k), lambda i, s: (i, 0)),
            scratch_shapes=[pltpu.VMEM((2, tc, k), F32), pltpu.VMEM((2, tc, k), F32),
                            pltpu.SemaphoreType.DMA((2,))],
        ),
        out_shape=jax.ShapeDtypeStruct((t, k), F32),
        compiler_params=_params("arbitrary"),
        name="moe_combine",
    )(slots, o_sorted, x)


def _ffn_kernel(te_ref, nsub_ref, x_ref, wg_ref, wu_ref, wd_ref, s_ref, *rest, nf, rb, has_resid):
    if has_resid:
        r_ref, o_ref = rest
    else:
        (o_ref,) = rest
    f = pl.program_id(1)

    @pl.when(f == 0)
    def _():
        o_ref[...] = jnp.zeros_like(o_ref)

    def row_block(r, carry):
        rows = pl.ds(pl.multiple_of(r * rb, rb), rb)
        x = x_ref[rows, :]
        gate = _dot(x, wg_ref[...].astype(BF16))
        up = _dot(x, wu_ref[...].astype(BF16))
        h = (gate * jax.nn.sigmoid(gate) * up).astype(BF16)
        o_ref[rows, :] += _dot(h, wd_ref[...].astype(BF16))
        return carry

    lax.fori_loop(0, nsub_ref[pl.program_id(0)], row_block, 0)

    @pl.when(f == nf - 1)
    def _():
        out = o_ref[...] * s_ref[...]
        if has_resid:
            out = r_ref[...] + out
        o_ref[...] = out


def grouped_ffn(x_rows, tile_expert, tile_nsub, scale, wg_all, wu_all, wd_all, lead, resid,
                *, tm, tf, rb):
    s, k = x_rows.shape
    ff = wg_all.shape[-1]
    nf = ff // tf
    n_tiles = s // tm
    nlead = len(lead)
    fblk = lambda i, f, act: jnp.where(act[i] > 0, f, nf - 1)
    w_in_spec = pl.BlockSpec((None,) * (nlead + 1) + (k, tf),
                             lambda i, f, te, act: lead + (te[i], 0, fblk(i, f, act)))
    w_out_spec = pl.BlockSpec((None,) * (nlead + 1) + (tf, k),
                              lambda i, f, te, act: lead + (te[i], fblk(i, f, act), 0))
    row_spec = lambda width: pl.BlockSpec((tm, width), lambda i, f, te, act: (i, 0))
    in_specs = [row_spec(k), w_in_spec, w_in_spec, w_out_spec, row_spec(1)]
    args = [x_rows, wg_all, wu_all, wd_all, scale]
    if resid is not None:
        in_specs.append(row_spec(k))
        args.append(resid)
    return pl.pallas_call(
        functools.partial(_ffn_kernel, nf=nf, rb=rb, has_resid=resid is not None),
        grid_spec=pltpu.PrefetchScalarGridSpec(
            num_scalar_prefetch=2,
            grid=(n_tiles, nf),
            in_specs=in_specs,
            out_specs=row_spec(k),
        ),
        out_shape=jax.ShapeDtypeStruct((s, k), F32),
        compiler_params=_params("arbitrary", "arbitrary"),
        name="grouped_ffn",
    )(tile_expert, tile_nsub, *args)


def _moe_plan(top_i, top_p, tm, rb, tg, n_tiles):
    t = top_i.shape[0]
    e = top_i.reshape(-1)
    p = top_p.reshape(-1)
    onehot = (e[:, None] == jnp.arange(N_EXPERTS, dtype=jnp.int32)[None, :]).astype(jnp.int32)
    csum = jnp.cumsum(onehot, axis=0)
    rank = jnp.take_along_axis(csum, e[:, None], axis=1)[:, 0] - 1
    counts = csum[-1]
    tiles_e = (counts + tm - 1) // tm
    tile_end = jnp.cumsum(tiles_e)
    tile_start = tile_end - tiles_e
    dest = tile_start[e] * tm + rank
    src_entry = jnp.full((n_tiles * tm,), -1, jnp.int32).at[dest].set(
        jnp.arange(2 * t, dtype=jnp.int32), unique_indices=True)
    entry = jnp.maximum(src_entry, 0)
    src_token = entry // 2
    scale = jnp.where(src_entry >= 0, p[entry], 0.0)
    tile_ids = jnp.arange(n_tiles, dtype=jnp.int32)
    tile_expert = jnp.sum((tile_ids[:, None] >= tile_end[None, :]).astype(jnp.int32), axis=1)
    active = (tile_ids < tile_end[-1]).astype(jnp.int32)
    last_expert = jnp.minimum(tile_expert[jnp.maximum(tile_end[-1] - 1, 0)], N_EXPERTS - 1)
    tile_expert = jnp.where(active > 0, jnp.minimum(tile_expert, N_EXPERTS - 1), last_expert)
    tile_valid = jnp.clip(counts[tile_expert] - (tile_ids - tile_start[tile_expert]) * tm, 0, tm) * active
    tile_nsub = (tile_valid + rb - 1) // rb
    per = tm // tg
    g_ids = jnp.arange(n_tiles * per, dtype=jnp.int32)
    g_active = (tile_valid[g_ids // per] > (g_ids % per) * tg).astype(jnp.int32)
    return (src_token, scale.reshape(-1, 1), dest.astype(jnp.int32), tile_expert.astype(jnp.int32),
            tile_nsub.astype(jnp.int32), g_active)


def _final_norm_kernel(x_ref, g_ref, op_ref, os_ref):
    i = pl.program_id(0)
    y = _rms_scale(x_ref[...]) * g_ref[...]

    @pl.when(i < T_PROMPT // CHUNK)
    def _():
        op_ref[...] = y

    @pl.when(i >= T_PROMPT // CHUNK)
    def _():
        os_ref[...] = y


def final_norm(x, g):
    n_prompt = T_PROMPT // CHUNK
    k = x.shape[1]

    def prompt_block(i):
        j = jnp.minimum(i, n_prompt - 1)
        return ((j % BATCH) * N_CHUNKS + j // BATCH, 0)

    return pl.pallas_call(
        _final_norm_kernel,
        grid=(T_ALL // CHUNK,),
        in_specs=[pl.BlockSpec((CHUNK, k), lambda i: (i, 0)),
                  pl.BlockSpec((1, k), lambda i: (0, 0))],
        out_specs=[pl.BlockSpec((CHUNK, k), prompt_block),
                   pl.BlockSpec((CHUNK, k), lambda i: (jnp.maximum(i - n_prompt, 0), 0))],
        out_shape=[jax.ShapeDtypeStruct((T_PROMPT, k), F32),
                   jax.ShapeDtypeStruct((T_SAMPLE, k), F32)],
        compiler_params=_params("arbitrary"),
        name="final_norm",
    )(x, g.reshape(1, k))


def _rope_tables(pos):
    half = HEAD_DIM // 2
    inv_freq = jnp.power(ROPE_THETA, -jnp.arange(half, dtype=F32) / half)
    ang = pos.astype(F32)[:, None] * inv_freq[None, :]
    cos, sin = jnp.cos(ang), jnp.sin(ang)
    return jnp.tile(cos, (1, 4)), jnp.tile(jnp.concatenate([-sin, sin], axis=1), (1, 2))


def _block_diag(blocks):
    *lead, n, r, c = blocks.shape
    eye = jnp.eye(n, dtype=blocks.dtype)
    return (blocks[..., :, :, None, :] * eye[:, None, :, None]).reshape(*lead, n * r, n * c)


def kernel(x_prompt, x_sample, cache_k, cache_v, state_ssm_re, state_ssm_im, norm1_g, w_in, attn_sinks, sgu_ln_g, sgu_ln_b, sgu_w, sgu_b, ssm_lambda_re, ssm_lambda_im, ssm_log_dt, ssm_b_re, ssm_b_im, ssm_c_re, ssm_c_im, ssm_d, ssm_glu_w, ssm_glu_b, mix_norm_g, w_out, norm2_g, ffn_w_gate, ffn_w_up, ffn_w_down, moe_router, moe_w_gate, moe_w_up, moe_w_down, final_norm_g):
    d = D_MODEL
    xp = x_prompt.reshape(BATCH, N_CHUNKS, CHUNK, d).transpose(1, 0, 2, 3).reshape(T_PROMPT, d)
    x = jnp.concatenate([xp, x_sample.reshape(T_SAMPLE, d)], axis=0)

    cos_p, sin_p = _rope_tables(jnp.arange(SEQ, dtype=jnp.int32))
    cos_s, sin_s = _rope_tables(PAST_LEN + jnp.arange(DEC_SEQ, dtype=jnp.int32))

    row3 = lambda a: a.reshape(DEPTH, 1, a.shape[-1])
    norm1_g3, norm2_g3, mix_g3 = row3(norm1_g), row3(norm2_g), row3(mix_norm_g)
    ln_g3, ln_b3 = row3(sgu_ln_g), row3(sgu_ln_b)
    glu_b3 = row3(ssm_glu_b)
    ssm_d3 = ssm_d.reshape(DEPTH, 1, D_SSM)
    cache_k4 = cache_k.reshape(DEPTH, DEC_BATCH, WINDOW, D_KV)
    cache_v4 = cache_v.reshape(DEPTH, DEC_BATCH, WINDOW, D_KV)
    w_in_b, w_out_b = w_in.astype(BF16), w_out.astype(BF16)

    reps = CHUNK // DEC_SEQ
    w_small = jnp.tril(sgu_w[:, :, :DEC_SEQ, :DEC_SEQ])
    eye = jnp.eye(reps, dtype=F32)
    w_sample = (eye[None, None, :, None, :, None] * w_small[:, :, None, :, None, :]).reshape(
        DEPTH, N_SGU_HEADS, CHUNK, CHUNK)
    w_mix = jnp.stack([sgu_w, w_sample], axis=1)
    b_prompt = jnp.repeat(sgu_b.transpose(0, 2, 1), SGU_HEAD_DIM, axis=2)
    b_sample = jnp.tile(b_prompt[:, :DEC_SEQ], (1, reps, 1))
    bias_mix = jnp.stack([b_prompt, b_sample], axis=1)

    a_re, a_im, bbt_re, bbt_im = ssm_discretize(
        ssm_lambda_re, ssm_lambda_im, ssm_log_dt,
        ssm_b_re.transpose(0, 1, 3, 2), ssm_b_im.transpose(0, 1, 3, 2))
    a_re3 = a_re.reshape(DEPTH, 1, N_STATE)
    a_im3 = a_im.reshape(DEPTH, 1, N_STATE)
    hg = N_SSM_GROUPS // 2
    halves = lambda blocks: _block_diag(blocks.reshape(DEPTH, 2, hg, *blocks.shape[2:]))
    bre, bim = halves(bbt_re), halves(bbt_im)
    cre = halves(ssm_c_re.transpose(0, 1, 3, 2))
    cim = halves(ssm_c_im.transpose(0, 1, 3, 2))
    zeros_state = jnp.zeros((BATCH, N_STATE), F32)

    ones_tiles = jnp.ones((T_ALL // ROW_TILE,), jnp.int32)
    ones_scale = jnp.ones((T_ALL, 1), F32)

    outs = {k: [] for k in ("kp", "vp", "hrp", "hip", "ks", "vs", "hrs", "his", "sgs")}
    for l in range(DEPTH):
        z = norm_matmul(x, norm1_g3, w_in_b, l, tm=ROW_TILE, tn=512)
        sinks = attn_sinks[l]
        attn_n, k_p, v_p = attn_prompt(z, sinks, cos_p, sin_p, mix_g3, l)
        attn_n, k_s, v_s = attn_sample(z, attn_n, cache_k4, cache_v4, sinks, cos_s, sin_s, mix_g3, l)
        sgu_n, sv_s = sgu(z, w_mix, bias_mix, ln_g3, ln_b3, mix_g3, l)
        ssm_args = (bre, bim, cre, cim, a_re3, a_im3, ssm_d3, ssm_glu_w, glu_b3, mix_g3, l)
        ssm_n, hr_p, hi_p = ssm(z, jnp.zeros((T_ALL, D_SSM), BF16), zeros_state, zeros_state, *ssm_args,
                                ns=BATCH, lc=CHUNK, first_block=0, n_blocks=N_CHUNKS)
        ssm_n, hr_s, hi_s = ssm(z, ssm_n, state_ssm_re[l].reshape(DEC_BATCH, N_STATE),
                                state_ssm_im[l].reshape(DEC_BATCH, N_STATE), *ssm_args,
                                ns=DEC_BATCH, lc=DEC_SEQ, first_block=T_PROMPT // T_SAMPLE, n_blocks=1)
        x = out_proj(attn_n, sgu_n, ssm_n, w_out_b, x, l, tm=ROW_TILE, tn=512)

        outs["kp"].append(k_p.reshape(BATCH, WINDOW, N_KV_HEADS, HEAD_DIM))
        outs["vp"].append(v_p.reshape(BATCH, WINDOW, N_KV_HEADS, HEAD_DIM))
        outs["hrp"].append(hr_p.reshape(BATCH, N_SSM_GROUPS, SSM_STATE))
        outs["hip"].append(hi_p.reshape(BATCH, N_SSM_GROUPS, SSM_STATE))
        outs["ks"].append(k_s.reshape(DEC_BATCH, WINDOW, N_KV_HEADS, HEAD_DIM))
        outs["vs"].append(v_s.reshape(DEC_BATCH, WINDOW, N_KV_HEADS, HEAD_DIM))
        outs["hrs"].append(hr_s.reshape(DEC_BATCH, N_SSM_GROUPS, SSM_STATE))
        outs["his"].append(hi_s.reshape(DEC_BATCH, N_SSM_GROUPS, SSM_STATE))
        outs["sgs"].append(sv_s.reshape(DEC_BATCH, DEC_SEQ, D_SGU))

        i = l // 2
        if l % 2 == 0:
            h2 = norm_rows(x, norm2_g3, l, tm=ROW_TILE, dtype=BF16)
            x = grouped_ffn(h2, jnp.full_like(ones_tiles, i), ones_tiles * (ROW_TILE // DENSE_ROW_BLOCK),
                            ones_scale, ffn_w_gate, ffn_w_up, ffn_w_down, (), x,
                            tm=ROW_TILE, tf=FF_TILE, rb=DENSE_ROW_BLOCK)
        else:
            top_i, top_p = router_top2(x, norm2_g3, moe_router, l, i, tm=ROW_TILE)
            src_token, scale, slots, tile_expert, tile_nsub, gather_active = _moe_plan(
                top_i[:, :2], top_p[:, :2], MOE_ROW_TILE, MOE_ROW_BLOCK, GATHER_TILE, MOE_TILES)
            xs = gather_norm(x, src_token, gather_active, norm2_g3, l, tg=GATHER_TILE)
            o_sorted = grouped_ffn(xs, tile_expert, tile_nsub, scale,
                                   moe_w_gate, moe_w_up, moe_w_down, (i,), None,
                                   tm=MOE_ROW_TILE, tf=FF_TILE, rb=MOE_ROW_BLOCK)
            x = moe_combine(o_sorted, slots, x, tc=GATHER_TILE)

    y_prompt, y_sample = final_norm(x, final_norm_g)
    st = lambda key: jnp.stack(outs[key])
    return (y_prompt.reshape(BATCH, SEQ, d), y_sample.reshape(DEC_BATCH, DEC_SEQ, d),
            st("kp"), st("vp"), st("hrp"), st("hip"), st("ks"), st("vs"), st("hrs"), st("his"), st("sgs"))
```
